```python
import jax, jax.numpy as jnp
from jax import lax
import numpy as np

D_MODEL = 4096
BATCH = 4
SEQ = 4096
DEPTH = 1

GLA_HEADS = 8
GLA_DK = 128
GLA_DV = 256
GLA_RANK = 16
GLA_TAU = 16.0
GLA_CHUNK = 64
FOX_HEADS = 16
FOX_DH = 128
FOX_BLOCK = 128
MEM_LEN = 256
XA_HEADS = 4
XA_DH = 256
D_FF = 11008
CONV_W = 3
EPS = 1e-6

GLA_QK = GLA_HEADS * GLA_DK
GLA_VW = GLA_HEADS * GLA_DV
FOX_W = FOX_HEADS * FOX_DH
SPLIT_SIZES = (GLA_QK, GLA_QK, GLA_VW, GLA_VW, GLA_RANK, FOX_W, FOX_W, FOX_W, FOX_HEADS, 2 * D_MODEL)
IN_COLS = sum(SPLIT_SIZES)

kernel_name = "hybrid_gla_fox_xattn_convffn"


def rmsnorm(x, g):
    xf = x.astype(jnp.float32)
    y = xf * lax.rsqrt(jnp.mean(xf * xf, axis=-1, keepdims=True) + EPS)
    return (y * g.astype(jnp.float32)).astype(x.dtype)


def split_cols(t, sizes):
    out, off = [], 0
    for s in sizes:
        out.append(t[..., off:off + s])
        off += s
    return out


def to_heads(t, n):
    b, s, _ = t.shape
    return t.reshape(b, s, n, -1).transpose(0, 2, 1, 3)


def from_heads(t):
    b, n, s, d = t.shape
    return t.transpose(0, 2, 1, 3).reshape(b, s, n * d)


def gla_chunked(q, k, v, lg):
    b, h, s, dk = q.shape
    dv = v.shape[-1]
    c = GLA_CHUNK
    n = s // c

    def chunks(t):
        return t.astype(jnp.float32).reshape(b, h, n, c, t.shape[-1]).transpose(2, 0, 1, 3, 4)

    qc, kc, vc, gc = chunks(q * (dk ** -0.5)), chunks(k), chunks(v), chunks(lg)
    bc = jnp.cumsum(gc, axis=-2)
    causal = jnp.tril(jnp.ones((c, c), dtype=bool))

    def step(state, inp):
        q_, k_, v_, b_ = inp
        o_inter = jnp.einsum("bhik,bhkv->bhiv", q_ * jnp.exp(b_), state)
        diff = b_[:, :, :, None, :] - b_[:, :, None, :, :]
        decay = jnp.exp(jnp.where(causal[:, :, None], diff, -jnp.inf))
        att = jnp.einsum("bhik,bhjk,bhijk->bhij", q_, k_, decay)
        o_intra = jnp.einsum("bhij,bhjv->bhiv", att, v_)
        b_last = b_[:, :, -1:, :]
        k_dec = k_ * jnp.exp(b_last - b_)
        new_state = jnp.exp(b_last)[:, :, 0, :, None] * state + jnp.einsum("bhjk,bhjv->bhkv", k_dec, v_)
        return new_state, o_inter + o_intra

    s0 = jnp.zeros((b, h, dk, dv), jnp.float32)
    _, o = lax.scan(step, s0, (qc, kc, vc, bc))
    return o.transpose(1, 2, 0, 3, 4).reshape(b, h, s, dv).astype(v.dtype)


def fox_attention(q, k, v, logf):
    b, h, s, d = q.shape
    nb = s // FOX_BLOCK
    cum = jnp.cumsum(logf, axis=-1)
    qb = q.reshape(b, h, nb, FOX_BLOCK, d).transpose(2, 0, 1, 3, 4)
    cb = cum.reshape(b, h, nb, FOX_BLOCK).transpose(2, 0, 1, 3)
    kpos = jnp.arange(s)
    scale = d ** -0.5

    def block(inp):
        q_, c_, i = inp
        qpos = i * FOX_BLOCK + jnp.arange(FOX_BLOCK)
        logits = jnp.einsum("bhqd,bhkd->bhqk", q_, k).astype(jnp.float32) * scale
        logits = logits + c_[..., :, None] - cum[:, :, None, :]
        logits = jnp.where(kpos[None, :] <= qpos[:, None], logits, -jnp.inf)
        p = jax.nn.softmax(logits, axis=-1)
        return jnp.einsum("bhqk,bhkd->bhqd", p.astype(v.dtype), v)

    o = lax.map(block, (qb, cb, jnp.arange(nb)))
    return o.transpose(1, 2, 0, 3, 4).reshape(b, h, s, d)


def hybrid_mixer(h, w_in, b_gate, w_gla_gate_up, b_gla_gate, g_gla_norm, b_fox_f,
                 w_gla_branch, w_fox_branch, w_out):
    bsz, s, _ = h.shape
    proj = h @ w_in
    (gq, gk, gv, gr, glow, fq, fk, fv, ff, gates) = split_cols(proj, SPLIT_SIZES)
    lg = jax.nn.log_sigmoid((glow @ w_gla_gate_up + b_gla_gate).astype(jnp.float32)) / GLA_TAU
    o_gla = gla_chunked(to_heads(gq, GLA_HEADS), to_heads(gk, GLA_HEADS),
                        to_heads(gv, GLA_HEADS), to_heads(lg, GLA_HEADS))
    o_gla = from_heads(rmsnorm(o_gla, g_gla_norm)) * jax.nn.silu(gr)
    y_a = o_gla @ w_gla_branch
    logf = jax.nn.log_sigmoid((ff + b_fox_f).astype(jnp.float32)).transpose(0, 2, 1)
    o_fox = fox_attention(to_heads(fq, FOX_HEADS), to_heads(fk, FOX_HEADS),
                          to_heads(fv, FOX_HEADS), logf)
    y_b = from_heads(o_fox) @ w_fox_branch
    g = jax.nn.sigmoid(gates + b_gate).reshape(bsz, s, 2, D_MODEL)
    return (g[..., 0, :] * y_a + g[..., 1, :] * y_b) @ w_out


def cross_attention(h, m, w_q, w_kv, w_o):
    bsz, s, _ = h.shape
    q = (h @ w_q).reshape(bsz, s, XA_HEADS, XA_DH)
    k, v = split_cols(m @ w_kv, (XA_HEADS * XA_DH, XA_HEADS * XA_DH))
    k = k.reshape(bsz, -1, XA_HEADS, XA_DH)
    v = v.reshape(bsz, -1, XA_HEADS, XA_DH)
    logits = jnp.einsum("bshd,bmhd->bhsm", q, k).astype(jnp.float32) * (XA_DH ** -0.5)
    p = jax.nn.softmax(logits, axis=-1)
    o = jnp.einsum("bhsm,bmhd->bshd", p.astype(v.dtype), v).reshape(bsz, s, XA_HEADS * XA_DH)
    return o @ w_o


def conv_ffn(h, w_up, w_conv, b_conv, w_down):
    u = h @ w_up
    ch = u.shape[-1]
    u = lax.conv_general_dilated(u, w_conv[:, None, :], window_strides=(1,),
                                 padding=[(CONV_W - 1, 0)],
                                 dimension_numbers=("NWC", "WIO", "NWC"),
                                 feature_group_count=ch) + b_conv
    gate, up = split_cols(u, (D_FF, D_FF))
    return (jax.nn.gelu(gate, approximate=True) * up) @ w_down


def setup_inputs(seed: int = 0) -> dict:
    key = jax.random.key(seed)
    ks = jax.random.split(key, 32)
    L = DEPTH

    def nrm(k, shape, scale):
        return jax.random.normal(k, shape, jnp.float32) * scale

    def gain(k, shape):
        return 1.0 + 0.02 * jax.random.normal(k, shape, jnp.float32)

    return {
        "x": nrm(ks[0], (BATCH, SEQ, D_MODEL), 1.0),
        "mem": nrm(ks[1], (BATCH, MEM_LEN, D_MODEL), 1.0),
        "g_mix_pre": gain(ks[2], (L, D_MODEL)),
        "w_in": nrm(ks[3], (L, D_MODEL, IN_COLS), D_MODEL ** -0.5),
        "b_gate": nrm(ks[4], (L, 2 * D_MODEL), 0.02),
        "w_gla_gate_up": nrm(ks[5], (L, GLA_RANK, GLA_QK), GLA_RANK ** -0.5),
        "b_gla_gate": nrm(ks[6], (L, GLA_QK), 0.02),
        "g_gla_norm": gain(ks[7], (L, GLA_DV)),
        "b_fox_f": 3.0 + 0.5 * jax.random.normal(ks[8], (L, FOX_HEADS), jnp.float32),
        "w_gla_branch": nrm(ks[9], (L, GLA_VW, D_MODEL), GLA_VW ** -0.5),
        "w_fox_branch": nrm(ks[10], (L, FOX_W, D_MODEL), FOX_W ** -0.5),
        "w_out": nrm(ks[11], (L, D_MODEL, D_MODEL), D_MODEL ** -0.5),
        "g_mix_post": gain(ks[12], (L, D_MODEL)),
        "g_xa_pre": gain(ks[13], (L, D_MODEL)),
        "g_mem": gain(ks[14], (L, D_MODEL)),
        "w_xa_q": nrm(ks[15], (L, D_MODEL, XA_HEADS * XA_DH), D_MODEL ** -0.5),
        "w_xa_kv": nrm(ks[16], (L, D_MODEL, 2 * XA_HEADS * XA_DH), D_MODEL ** -0.5),
        "w_xa_o": nrm(ks[17], (L, XA_HEADS * XA_DH, D_MODEL), (XA_HEADS * XA_DH) ** -0.5),
        "g_xa_post": gain(ks[18], (L, D_MODEL)),
        "g_ffn_pre": gain(ks[19], (L, D_MODEL)),
        "w_ffn_up": nrm(ks[20], (L, D_MODEL, 2 * D_FF), D_MODEL ** -0.5),
        "w_conv": nrm(ks[21], (L, CONV_W, 2 * D_FF), CONV_W ** -0.5),
        "b_conv": nrm(ks[22], (L, 2 * D_FF), 0.02),
        "w_ffn_down": nrm(ks[23], (L, D_FF, D_MODEL), D_FF ** -0.5),
        "g_ffn_post": gain(ks[24], (L, D_MODEL)),
    }


def reference(x, mem, g_mix_pre, w_in, b_gate, w_gla_gate_up, b_gla_gate, g_gla_norm, b_fox_f,
              w_gla_branch, w_fox_branch, w_out, g_mix_post, g_xa_pre, g_mem, w_xa_q, w_xa_kv,
              w_xa_o, g_xa_post, g_ffn_pre, w_ffn_up, w_conv, b_conv, w_ffn_down, g_ffn_post):
    for l in range(DEPTH):
        h = rmsnorm(x, g_mix_pre[l])
        y = hybrid_mixer(h, w_in[l], b_gate[l], w_gla_gate_up[l], b_gla_gate[l], g_gla_norm[l],
                         b_fox_f[l], w_gla_branch[l], w_fox_branch[l], w_out[l])
        x = x + rmsnorm(y, g_mix_post[l])
        h = rmsnorm(x, g_xa_pre[l])
        m = rmsnorm(mem, g_mem[l])
        y = cross_attention(h, m, w_xa_q[l], w_xa_kv[l], w_xa_o[l])
        x = x + rmsnorm(y, g_xa_post[l])
        h = rmsnorm(x, g_ffn_pre[l])
        y = conv_ffn(h, w_ffn_up[l], w_conv[l], b_conv[l], w_ffn_down[l])
        x = x + rmsnorm(y, g_ffn_post[l])
    return x
```

```python
import functools
import math

import jax
import jax.numpy as jnp
from jax import lax
from jax.experimental import pallas as pl
from jax.experimental.pallas import tpu as pltpu

F32 = jnp.float32
BF16 = jnp.bfloat16

LANES = 128
V7X_VMEM_BYTES = 64 * 1024 * 1024
VMEM_CAP_BYTES = V7X_VMEM_BYTES - 8 * 1024 * 1024

EPS = 1e-6
GLA_TAU = 16.0
GLA_CHUNK = 64
GLA_SUB = 16
GLA_RANK = 16
XA_HEADS = 4
GELU_C = math.sqrt(2.0 / math.pi)
NEG_BIG = -1e30


def _params(semantics, block_bytes):
    limit = min(VMEM_CAP_BYTES, 2 * block_bytes + 16 * 1024 * 1024)
    return pltpu.CompilerParams(dimension_semantics=semantics, vmem_limit_bytes=int(limit))


def _nbytes(shape, dtype):
    return math.prod(shape) * jnp.dtype(dtype).itemsize


def _pick_block(target, *dims):
    g = math.gcd(*dims)
    best = None
    for b in range(LANES, min(target, g) + 1, LANES):
        if g % b == 0:
            best = b
    assert best is not None, (target, dims)
    return best


def _log_sigmoid(z):
    return jnp.minimum(z, 0.0) - jnp.log1p(jnp.exp(-jnp.abs(z)))


def _sigmoid(z):
    return 1.0 / (1.0 + jnp.exp(-z))


def _tril_cumsum(tril, x):
    hi = x.astype(BF16)
    r1 = x - hi.astype(F32)
    mid = r1.astype(BF16)
    lo = (r1 - mid.astype(F32)).astype(BF16)
    dot = functools.partial(jnp.dot, preferred_element_type=F32)
    return dot(tril, hi) + dot(tril, mid) + dot(tril, lo)


def _dot_nt(a, b):
    return lax.dot_general(a, b, (((1,), (1,)), ((), ())), preferred_element_type=F32)


def _dot_tn(a, b):
    return lax.dot_general(a, b, (((0,), (0,)), ((), ())), preferred_element_type=F32)


def _rmsnorm_kernel(x_ref, g_ref, o_ref):
    x = x_ref[...]
    y = x * lax.rsqrt(jnp.mean(x * x, axis=-1, keepdims=True) + EPS)
    o_ref[...] = (y * g_ref[...]).astype(o_ref.dtype)


def _rmsnorm(x, g, out_dtype=BF16, bm=256):
    m, d = x.shape
    blk = _nbytes((bm, d), F32) + _nbytes((bm, d), out_dtype)
    return pl.pallas_call(
        _rmsnorm_kernel,
        grid=(m // bm,),
        in_specs=[pl.BlockSpec((bm, d), lambda i: (i, 0)), pl.BlockSpec((1, d), lambda i: (0, 0))],
        out_specs=pl.BlockSpec((bm, d), lambda i: (i, 0)),
        out_shape=jax.ShapeDtypeStruct((m, d), out_dtype),
        compiler_params=_params(("parallel",), blk),
        name="rmsnorm",
    )(x, g.reshape(1, d))


def _norm_resid_kernel(y_ref, x_ref, g_ref, o_ref):
    y = y_ref[...]
    n = y * lax.rsqrt(jnp.mean(y * y, axis=-1, keepdims=True) + EPS)
    o_ref[...] = x_ref[...] + n * g_ref[...]


def _norm_resid(y, x, g, bm=256):
    m, d = x.shape
    blk = 3 * _nbytes((bm, d), F32)
    row = pl.BlockSpec((bm, d), lambda i: (i, 0))
    return pl.pallas_call(
        _norm_resid_kernel,
        grid=(m // bm,),
        in_specs=[row, row, pl.BlockSpec((1, d), lambda i: (0, 0))],
        out_specs=row,
        out_shape=jax.ShapeDtypeStruct((m, d), F32),
        compiler_params=_params(("parallel",), blk),
        name="norm_resid",
    )(y, x, g.reshape(1, d))


def _mm_kernel(a_ref, w_ref, o_ref):
    o_ref[...] = jnp.dot(a_ref[...], w_ref[...], preferred_element_type=F32).astype(o_ref.dtype)


def _matmul(a, w, out_dtype, bm, bn, name):
    m, k = a.shape
    n = w.shape[1]
    bm, bn = _pick_block(bm, m), _pick_block(bn, n)
    blk = _nbytes((bm, k), a.dtype) + _nbytes((k, bn), w.dtype) + _nbytes((bm, bn), out_dtype)
    return pl.pallas_call(
        _mm_kernel,
        grid=(m // bm, n // bn),
        in_specs=[pl.BlockSpec((bm, k), lambda i, j: (i, 0)), pl.BlockSpec((k, bn), lambda i, j: (0, j))],
        out_specs=pl.BlockSpec((bm, bn), lambda i, j: (i, j)),
        out_shape=jax.ShapeDtypeStruct((m, n), out_dtype),
        compiler_params=_params(("parallel", "arbitrary"), blk),
        name=name,
    )(a, w)


def _fox_bias_kernel(sm_ref, bias_ref, ct_ref, ctt_ref):
    blk = LANES
    n_blk = sm_ref.shape[0] // blk
    r_i = lax.broadcasted_iota(jnp.int32, (blk, blk), 0)
    c_i = lax.broadcasted_iota(jnp.int32, (blk, blk), 1)
    tril = (r_i >= c_i).astype(BF16)

    def body(i, carry):
        start = pl.multiple_of(i * blk, blk)
        z = sm_ref[pl.ds(start, blk), :] + bias_ref[...]
        cs = _tril_cumsum(tril, _log_sigmoid(z)) + carry
        ct_ref[pl.ds(start, blk), :] = cs
        ctt_ref[:, pl.ds(start, blk)] = cs.T
        return cs[blk - 1:blk, :]

    lax.fori_loop(0, n_blk, body, jnp.zeros((1, blk), F32))


def _fox_bias(small, bias_row, batch, seq):
    tok = pl.BlockSpec((seq, LANES), lambda b: (b, 0))
    return pl.pallas_call(
        _fox_bias_kernel,
        grid=(batch,),
        in_specs=[tok, pl.BlockSpec((1, LANES), lambda b: (0, 0))],
        out_specs=[tok, pl.BlockSpec((None, LANES, seq), lambda b: (b, 0, 0))],
        out_shape=[jax.ShapeDtypeStruct((batch * seq, LANES), F32),
                   jax.ShapeDtypeStruct((batch, LANES, seq), F32)],
        compiler_params=_params(("parallel",), 3 * _nbytes((seq, LANES), F32)),
        name="fox_bias",
    )(small, bias_row)


def _fox_kernel(qi_tab, kj_tab, q_ref, k_ref, v_ref, cq_ref, ck_ref, o_ref, m_scr, l_scr, acc_scr, cq_scr):
    head = pl.program_id(1)
    step = pl.program_id(2)
    qi = qi_tab[step]
    kj = kj_tab[step]
    bq, dh = q_ref.shape
    bk = k_ref.shape[0]

    @pl.when(kj == 0)
    def _():
        m_scr[...] = jnp.full_like(m_scr, NEG_BIG)
        l_scr[...] = jnp.zeros_like(l_scr)
        acc_scr[...] = jnp.zeros_like(acc_scr)
        lane = lax.broadcasted_iota(jnp.int32, cq_ref.shape, 1)
        cq_scr[...] = jnp.sum(jnp.where(lane == head, cq_ref[...], 0.0), axis=1, keepdims=True)

    s = _dot_nt(q_ref[...], k_ref[...]) * (dh ** -0.5)
    s = s + cq_scr[...] - ck_ref[...]
    row = lax.broadcasted_iota(jnp.int32, (bq, bk), 0) + qi * bq
    col = lax.broadcasted_iota(jnp.int32, (bq, bk), 1) + kj * bk
    s = jnp.where(col <= row, s, NEG_BIG)
    m_prev = m_scr[...]
    m_new = jnp.maximum(m_prev, jnp.max(s, axis=1, keepdims=True))
    alpha = jnp.exp(m_prev - m_new)
    p = jnp.exp(s - m_new)
    l_scr[...] = alpha * l_scr[...] + jnp.sum(p, axis=1, keepdims=True)
    acc_scr[...] = alpha * acc_scr[...] + jnp.dot(p.astype(BF16), v_ref[...], preferred_element_type=F32)
    m_scr[...] = m_new

    @pl.when(kj * bk + bk >= qi * bq + bq)
    def _():
        o_ref[...] = (acc_scr[...] / l_scr[...]).astype(o_ref.dtype)


def _fox_attention(proj, cum_tok, cum_head, batch, seq, heads, dh, col_q, col_k, col_v, bq=512):
    bk = bq
    nq = seq // bq
    pairs = [(i, j) for i in range(nq) for j in range(i + 1)]
    qi_tab = jnp.asarray([p[0] for p in pairs], jnp.int32)
    kj_tab = jnp.asarray([p[1] for p in pairs], jnp.int32)
    cq0, ck0, cv0 = col_q // dh, col_k // dh, col_v // dh

    grid_spec = pltpu.PrefetchScalarGridSpec(
        num_scalar_prefetch=2,
        grid=(batch, heads, len(pairs)),
        in_specs=[
            pl.BlockSpec((bq, dh), lambda b, h, s, qt, kt: (b * nq + qt[s], cq0 + h)),
            pl.BlockSpec((bk, dh), lambda b, h, s, qt, kt: (b * nq + kt[s], ck0 + h)),
            pl.BlockSpec((bk, dh), lambda b, h, s, qt, kt: (b * nq + kt[s], cv0 + h)),
            pl.BlockSpec((bq, LANES), lambda b, h, s, qt, kt: (b * nq + qt[s], 0)),
            pl.BlockSpec((None, None, 1, bk), lambda b, h, s, qt, kt: (b, h, 0, kt[s])),
        ],
        out_specs=pl.BlockSpec((bq, dh), lambda b, h, s, qt, kt: (b * nq + qt[s], h)),
        scratch_shapes=[pltpu.VMEM((bq, 1), F32), pltpu.VMEM((bq, 1), F32),
                        pltpu.VMEM((bq, dh), F32), pltpu.VMEM((bq, 1), F32)],
    )
    blk = 4 * _nbytes((bq, dh), BF16) + _nbytes((bq, LANES), F32) + 4 * _nbytes((bq, bk), F32)
    return pl.pallas_call(
        _fox_kernel,
        grid_spec=grid_spec,
        out_shape=jax.ShapeDtypeStruct((batch * seq, heads * dh), BF16),
        compiler_params=_params(("parallel", "parallel", "arbitrary"), blk),
        name="fox_attention",
    )(qi_tab, kj_tab, proj, proj, proj, cum_tok, cum_head.reshape(batch, LANES, 1, seq))


def _gla_kernel(q_ref, k_ref, v_ref, r_ref, low_ref, wup_ref, bup_ref, gn_ref, o_ref, st_ref, kf_ref, b_ref):
    c, sb = GLA_CHUNK, GLA_SUB
    n_sub = c // sb
    seq, dk = q_ref.shape
    scale = dk ** -0.5
    st_ref[...] = jnp.zeros_like(st_ref)
    r_i = lax.broadcasted_iota(jnp.int32, (c, c), 0)
    c_i = lax.broadcasted_iota(jnp.int32, (c, c), 1)
    tril = (r_i >= c_i).astype(BF16)
    sub_shift = sb.bit_length() - 1
    sub_row = jnp.right_shift(r_i, sub_shift)
    sub_col = jnp.right_shift(c_i, sub_shift)
    lane_sb = lax.broadcasted_iota(jnp.int32, (sb, c), 1)
    row_sb = lax.broadcasted_iota(jnp.int32, (sb, c), 0)

    def chunk(ci, carry):
        rows = pl.ds(pl.multiple_of(ci * c, c), c)
        z = jnp.dot(low_ref[rows, :].astype(BF16), wup_ref[...], preferred_element_type=F32) + bup_ref[...]
        lg = _log_sigmoid(z) * (1.0 / GLA_TAU)
        b = _tril_cumsum(tril, lg)
        qf = q_ref[rows, :].astype(F32) * scale
        kf = k_ref[rows, :].astype(F32)
        v = v_ref[rows, :]
        b_last = b[c - 1:c, :]
        st = st_ref[...]

        o = _dot_nt((qf * jnp.exp(b)).astype(BF16), st.astype(BF16))

        kf_ref[...] = kf
        b_ref[...] = b
        ends = [jnp.broadcast_to(b_ref[(j + 1) * sb - 1:(j + 1) * sb, :], (c, dk)) for j in range(n_sub)]
        end_of_row = jnp.concatenate([e[:sb] for e in ends], axis=0)
        k_hat = (kf * jnp.exp(end_of_row - b)).astype(BF16)
        q_stack = jnp.concatenate(
            [(qf * jnp.exp(jnp.minimum(b - ends[j], 0.0))).astype(BF16) for j in range(n_sub - 1)], axis=0)
        r = _dot_nt(q_stack, k_hat)
        att = jnp.zeros((c, c), F32)
        for j in range(n_sub - 1):
            att = jnp.where((sub_col == j) & (sub_row > j), r[j * c:(j + 1) * c, :], att)

        diag = []
        for i in range(n_sub):
            q_i = qf[i * sb:(i + 1) * sb, :]
            b_i = b[i * sb:(i + 1) * sb, :]
            tile = jnp.zeros((sb, c), F32)
            for jj in range(sb):
                j = i * sb + jj
                dec = jnp.exp(jnp.minimum(b_i - b_ref[j:j + 1, :], 0.0))
                s = jnp.sum(q_i * kf_ref[j:j + 1, :] * dec, axis=-1, keepdims=True)
                tile = jnp.where(lane_sb == j, jnp.where(row_sb >= jj, s, 0.0), tile)
            diag.append(tile)
        att = att + jnp.concatenate(diag, axis=0)

        o = o + jnp.dot(att.astype(BF16), v, preferred_element_type=F32)
        k_dec = (kf * jnp.exp(b_last - b)).astype(BF16)
        st_ref[...] = st * jnp.exp(b_last) + _dot_tn(v, k_dec)

        n = o * lax.rsqrt(jnp.mean(o * o, axis=-1, keepdims=True) + EPS) * gn_ref[...]
        gate = r_ref[rows, :].astype(F32)
        o_ref[rows, :] = (n * (gate * _sigmoid(gate))).astype(o_ref.dtype)
        return carry

    lax.fori_loop(0, seq // c, chunk, 0)


def _gla(proj, small, wup_pad, b_up, g_norm, batch, seq, heads, dk, dv, col_q, col_k, col_v, col_r):
    cq0, ck0, cv0, cr0 = col_q // dk, col_k // dk, col_v // dv, col_r // dv
    blk = 2 * _nbytes((seq, dk), BF16) + 3 * _nbytes((seq, dv), BF16) + _nbytes((seq, LANES), F32)
    return pl.pallas_call(
        _gla_kernel,
        grid=(batch, heads),
        in_specs=[
            pl.BlockSpec((seq, dk), lambda b, h: (b, cq0 + h)),
            pl.BlockSpec((seq, dk), lambda b, h: (b, ck0 + h)),
            pl.BlockSpec((seq, dv), lambda b, h: (b, cv0 + h)),
            pl.BlockSpec((seq, dv), lambda b, h: (b, cr0 + h)),
            pl.BlockSpec((seq, LANES), lambda b, h: (b, 0)),
            pl.BlockSpec((LANES, dk), lambda b, h: (0, h)),
            pl.BlockSpec((1, dk), lambda b, h: (0, h)),
            pl.BlockSpec((1, dv), lambda b, h: (0, 0)),
        ],
        out_specs=pl.BlockSpec((seq, dv), lambda b, h: (b, h)),
        out_shape=jax.ShapeDtypeStruct((batch * seq, heads * dv), BF16),
        scratch_shapes=[pltpu.VMEM((dv, dk), F32), pltpu.VMEM((GLA_CHUNK, dk), F32),
                        pltpu.VMEM((GLA_CHUNK, dk), F32)],
        compiler_params=_params(("parallel", "parallel"), blk),
        name="gla",
    )(proj, proj, proj, proj, small, wup_pad, b_up, g_norm)


def _merge_kernel(a_ref, wa_ref, f_ref, wf_ref, g0_ref, g1_ref, b0_ref, b1_ref, o_ref):
    ya = jnp.dot(a_ref[...], wa_ref[...], preferred_element_type=F32)
    yb = jnp.dot(f_ref[...], wf_ref[...], preferred_element_type=F32)
    g0 = _sigmoid(g0_ref[...].astype(F32) + b0_ref[...])
    g1 = _sigmoid(g1_ref[...].astype(F32) + b1_ref[...])
    o_ref[...] = (g0 * ya + g1 * yb).astype(o_ref.dtype)


def _merge(o_gla, w_a, o_fox, w_f, proj, b_gate, col_gates, bm=1024, bn=512):
    m, ka = o_gla.shape
    kf = o_fox.shape[1]
    d = w_a.shape[1]
    bm, bn = _pick_block(bm, m), _pick_block(bn, d, col_gates)
    g0, nb = col_gates // bn, d // bn
    blk = (_nbytes((bm, ka), BF16) + _nbytes((bm, kf), BF16) + _nbytes((ka, bn), BF16)
           + _nbytes((kf, bn), BF16) + 3 * _nbytes((bm, bn), BF16))
    return pl.pallas_call(
        _merge_kernel,
        grid=(m // bm, nb),
        in_specs=[
            pl.BlockSpec((bm, ka), lambda i, j: (i, 0)),
            pl.BlockSpec((ka, bn), lambda i, j: (0, j)),
            pl.BlockSpec((bm, kf), lambda i, j: (i, 0)),
            pl.BlockSpec((kf, bn), lambda i, j: (0, j)),
            pl.BlockSpec((bm, bn), lambda i, j: (i, g0 + j)),
            pl.BlockSpec((bm, bn), lambda i, j: (i, g0 + nb + j)),
            pl.BlockSpec((1, bn), lambda i, j: (0, j)),
            pl.BlockSpec((1, bn), lambda i, j: (0, nb + j)),
        ],
        out_specs=pl.BlockSpec((bm, bn), lambda i, j: (i, j)),
        out_shape=jax.ShapeDtypeStruct((m, d), BF16),
        compiler_params=_params(("parallel", "arbitrary"), blk),
        name="branch_merge",
    )(o_gla, w_a, o_fox, w_f, proj, proj, b_gate, b_gate)


def _xattn_kernel(q_ref, kv_ref, o_ref, *, heads, dh):
    for h in range(heads):
        q = q_ref[:, h * dh:(h + 1) * dh]
        k = kv_ref[:, h * dh:(h + 1) * dh]
        v = kv_ref[:, (heads + h) * dh:(heads + h + 1) * dh]
        s = _dot_nt(q, k) * (dh ** -0.5)
        p = jnp.exp(s - jnp.max(s, axis=-1, keepdims=True))
        l = jnp.sum(p, axis=-1, keepdims=True)
        o = jnp.dot(p.astype(BF16), v, preferred_element_type=F32) / l
        o_ref[:, h * dh:(h + 1) * dh] = o.astype(o_ref.dtype)


def _xattn(q, kv, batch, seq, mem_len, heads, dh, bq=512):
    nq = seq // bq
    w = heads * dh
    blk = 2 * _nbytes((bq, w), BF16) + _nbytes((mem_len, 2 * w), BF16) + 4 * _nbytes((bq, mem_len), F32)
    return pl.pallas_call(
        functools.partial(_xattn_kernel, heads=heads, dh=dh),
        grid=(batch, nq),
        in_specs=[pl.BlockSpec((bq, w), lambda b, i: (b * nq + i, 0)),
                  pl.BlockSpec((mem_len, 2 * w), lambda b, i: (b, 0))],
        out_specs=pl.BlockSpec((bq, w), lambda b, i: (b * nq + i, 0)),
        out_shape=jax.ShapeDtypeStruct((batch * seq, w), BF16),
        compiler_params=_params(("parallel", "arbitrary"), blk),
        name="cross_attention",
    )(q, kv)


CONV_ROWS = 256
CONV_HALO = 16


def _conv_geglu_kernel(ug_ref, uu_ref, wg_ref, wu_ref, bg_ref, bu_ref, o_ref):
    seq = ug_ref.shape[0]

    def conv(u_ref, w_ref, b_ref, start):
        cur = u_ref[pl.ds(start, CONV_ROWS), :].astype(F32)
        prev = pl.multiple_of(jnp.maximum(start - CONV_HALO, 0), CONV_HALO)
        halo = u_ref[pl.ds(prev, CONV_HALO), :].astype(F32)
        halo = jnp.where(start > 0, halo, 0.0)
        ext = jnp.concatenate([halo, cur], axis=0)
        x1 = pltpu.roll(ext, 1, 0)[CONV_HALO:]
        x2 = pltpu.roll(ext, 2, 0)[CONV_HALO:]
        return w_ref[0:1, :] * x2 + w_ref[1:2, :] * x1 + w_ref[2:3, :] * cur + b_ref[...]

    def body(i, carry):
        start = pl.multiple_of(i * CONV_ROWS, CONV_ROWS)
        g = conv(ug_ref, wg_ref, bg_ref, start)
        u = conv(uu_ref, wu_ref, bu_ref, start)
        gelu = 0.5 * g * (1.0 + jnp.tanh(GELU_C * (g + 0.044715 * (g * g * g))))
        o_ref[pl.ds(start, CONV_ROWS), :] = (gelu * u).astype(o_ref.dtype)
        return carry

    lax.fori_loop(0, seq // CONV_ROWS, body, 0)


def _conv_geglu(u, w_conv, b_conv, batch, seq, d_ff, bn=256):
    bn = _pick_block(bn, d_ff)
    nb = d_ff // bn
    col = lambda off: pl.BlockSpec((seq, bn), lambda b, j: (b, off + j))
    wsp = lambda off: pl.BlockSpec((w_conv.shape[0], bn), lambda b, j: (0, off + j))
    bsp = lambda off: pl.BlockSpec((1, bn), lambda b, j: (0, off + j))
    return pl.pallas_call(
        _conv_geglu_kernel,
        grid=(batch, nb),
        in_specs=[col(0), col(nb), wsp(0), wsp(nb), bsp(0), bsp(nb)],
        out_specs=pl.BlockSpec((seq, bn), lambda b, j: (b, j)),
        out_shape=jax.ShapeDtypeStruct((batch * seq, d_ff), BF16),
        compiler_params=_params(("parallel", "arbitrary"), 3 * _nbytes((seq, bn), BF16)),
        name="conv_geglu",
    )(u, u, w_conv, w_conv, b_conv, b_conv)


def _layer(x, mem, g_mix_pre, w_in, b_gate, w_gla_gate_up, b_gla_gate, g_gla_norm, b_fox_f,
           w_gla_branch, w_fox_branch, w_out, g_mix_post, g_xa_pre, g_mem, w_xa_q, w_xa_kv,
           w_xa_o, g_xa_post, g_ffn_pre, w_ffn_up, w_conv, b_conv, w_ffn_down, g_ffn_post,
           batch, seq):
    d_model = x.shape[1]
    gla_qk = w_gla_gate_up.shape[1]
    gla_dv = g_gla_norm.shape[0]
    gla_vw = w_gla_branch.shape[0]
    gla_heads = gla_vw // gla_dv
    gla_dk = gla_qk // gla_heads
    fox_w = w_fox_branch.shape[0]
    fox_heads = b_fox_f.shape[0]
    fox_dh = fox_w // fox_heads
    xa_w = w_xa_q.shape[1]
    mem_len = mem.shape[0] // batch
    xa_heads = XA_HEADS
    xa_dh = xa_w // xa_heads
    d_ff = w_ffn_down.shape[0]
    assert fox_heads + GLA_RANK <= LANES and gla_dk == LANES and fox_dh == LANES

    sizes = (gla_qk, gla_qk, gla_vw, gla_vw, GLA_RANK, fox_w, fox_w, fox_w, fox_heads, 2 * d_model)
    offs = [0]
    for s in sizes:
        offs.append(offs[-1] + s)
    o_gq, o_gk, o_gv, o_gr, o_low, o_fq, o_fk, o_fv, o_ff, o_gates = offs[:-1]
    w_big = jnp.concatenate([w_in[:, :o_low], w_in[:, o_fq:o_ff], w_in[:, o_gates:]], axis=1).astype(BF16)
    w_small = jnp.zeros((d_model, LANES), F32)
    w_small = w_small.at[:, :fox_heads].set(w_in[:, o_ff:o_ff + fox_heads])
    w_small = w_small.at[:, fox_heads:fox_heads + GLA_RANK].set(w_in[:, o_low:o_low + GLA_RANK]).astype(BF16)
    c_gq, c_gk, c_gv, c_gr = o_gq, o_gk, o_gv, o_gr
    c_fq = o_low
    c_fk, c_fv = c_fq + fox_w, c_fq + 2 * fox_w
    c_gates = c_fq + 3 * fox_w
    wup_pad = jnp.zeros((LANES, gla_qk), F32).at[fox_heads:fox_heads + GLA_RANK].set(w_gla_gate_up).astype(BF16)
    fox_bias_row = jnp.zeros((1, LANES), F32).at[0, :fox_heads].set(b_fox_f)

    h = _rmsnorm(x, g_mix_pre)
    proj = _matmul(h, w_big, BF16, 1024, 1024, "in_proj")
    small = _matmul(h, w_small, F32, 1024, LANES, "in_proj_small")
    o_gla = _gla(proj, small, wup_pad, b_gla_gate.reshape(1, -1), g_gla_norm.reshape(1, -1),
                 batch, seq, gla_heads, gla_dk, gla_dv, c_gq, c_gk, c_gv, c_gr)
    cum_tok, cum_head = _fox_bias(small, fox_bias_row, batch, seq)
    o_fox = _fox_attention(proj, cum_tok, cum_head, batch, seq, fox_heads, fox_dh, c_fq, c_fk, c_fv)
    merged = _merge(o_gla, w_gla_branch.astype(BF16), o_fox, w_fox_branch.astype(BF16), proj,
                    b_gate.reshape(1, -1), c_gates)
    y = _matmul(merged, w_out.astype(BF16), F32, 1024, 1024, "mix_out")
    x = _norm_resid(y, x, g_mix_post)

    h = _rmsnorm(x, g_xa_pre)
    m = _rmsnorm(mem, g_mem)
    q = _matmul(h, w_xa_q.astype(BF16), BF16, 1024, 1024, "xa_q")
    kv = _matmul(m, w_xa_kv.astype(BF16), BF16, 1024, 1024, "xa_kv")
    o = _xattn(q, kv, batch, seq, mem_len, xa_heads, xa_dh)
    y = _matmul(o, w_xa_o.astype(BF16), F32, 1024, 1024, "xa_out")
    x = _norm_resid(y, x, g_xa_post)

    h = _rmsnorm(x, g_ffn_pre)
    u = _matmul(h, w_ffn_up.astype(BF16), BF16, 1024, 512, "ffn_up")
    act = _conv_geglu(u, w_conv, b_conv.reshape(1, -1), batch, seq, d_ff)
    y = _matmul(act, w_ffn_down.astype(BF16), F32, 512, 512, "ffn_down")
    return _norm_resid(y, x, g_ffn_post)


def kernel(x, mem, g_mix_pre, w_in, b_gate, w_gla_gate_up, b_gla_gate, g_gla_norm, b_fox_f, w_gla_branch, w_fox_branch, w_out, g_mix_post, g_xa_pre, g_mem, w_xa_q, w_xa_kv, w_xa_o, g_xa_post, g_ffn_pre, w_ffn_up, w_conv, b_conv, w_ffn_down, g_ffn_post):
    batch, seq, d_model = x.shape
    xf = x.reshape(batch * seq, d_model)
    mf = mem.reshape(-1, d_model)
    per_layer = (g_mix_pre, w_in, b_gate, w_gla_gate_up, b_gla_gate, g_gla_norm, b_fox_f, w_gla_branch,
                 w_fox_branch, w_out, g_mix_post, g_xa_pre, g_mem, w_xa_q, w_xa_kv, w_xa_o, g_xa_post,
                 g_ffn_pre, w_ffn_up, w_conv, b_conv, w_ffn_down, g_ffn_post)
    for l in range(w_in.shape[0]):
        xf = _layer(xf, mf, *(p[l] for p in per_layer), batch=batch, seq=seq)
    return xf.reshape(batch, seq, d_model)
```

```python
import functools
import math

import jax
import jax.numpy as jnp
from jax import lax
from jax.experimental import pallas as pl
from jax.experimental.pallas import tpu as pltpu

F32 = jnp.float32
BF16 = jnp.bfloat16

LANES = 128
SUBLANES = 8
V7X_VMEM_BYTES = 64 * 1024 * 1024
VMEM_CAP_BYTES = V7X_VMEM_BYTES - 8 * 1024 * 1024

EPS = 1e-6
GLA_TAU = 16.0
GLA_CHUNK = 64
GLA_SUB = 16
GLA_RANK = 16
GLA_GROUP = 4
GLA_ROWS = 1024
FOX_GROUP = 4
FOX_EXTRA = 6
XA_HEADS = 4
GELU_C = math.sqrt(2.0 / math.pi)
LOG2E = math.log2(math.e)
NEG_BIG = -1e30


def _params(semantics, block_bytes):
    limit = min(VMEM_CAP_BYTES, 2 * block_bytes + 16 * 1024 * 1024)
    return pltpu.CompilerParams(dimension_semantics=semantics, vmem_limit_bytes=int(limit))


def _nbytes(shape, dtype):
    return math.prod(shape) * jnp.dtype(dtype).itemsize


def _pick_block(target, *dims):
    g = math.gcd(*dims)
    best = None
    for b in range(LANES, min(target, g) + 1, LANES):
        if g % b == 0:
            best = b
    assert best is not None, (target, dims)
    return best


def _log_sigmoid(z):
    return jnp.minimum(z, 0.0) - jnp.log1p(jnp.exp(-jnp.abs(z)))


def _sigmoid(z):
    return 1.0 / (1.0 + jnp.exp(-z))


def _split3(x):
    hi = x.astype(BF16)
    r1 = x - hi.astype(F32)
    mid = r1.astype(BF16)
    lo = (r1 - mid.astype(F32)).astype(BF16)
    return hi, mid, lo


def _tril_cumsum(tril, x):
    hi, mid, lo = _split3(x)
    dot = functools.partial(jnp.dot, preferred_element_type=F32)
    return dot(tril, hi) + dot(tril, mid) + dot(tril, lo)


def _dot_nt(a, b):
    return lax.dot_general(a, b, (((1,), (1,)), ((), ())), preferred_element_type=F32)


def _dot_tn(a, b):
    return lax.dot_general(a, b, (((0,), (0,)), ((), ())), preferred_element_type=F32)


def _rms_scale(y):
    return y * lax.rsqrt(jnp.mean(y * y, axis=-1, keepdims=True) + EPS)


def _rmsnorm_kernel(x_ref, g_ref, o_ref):
    o_ref[...] = (_rms_scale(x_ref[...]) * g_ref[...]).astype(o_ref.dtype)


def _rmsnorm(x, g, bm=256):
    m, d = x.shape
    blk = _nbytes((bm, d), F32) + _nbytes((bm, d), BF16)
    return pl.pallas_call(
        _rmsnorm_kernel,
        grid=(m // bm,),
        in_specs=[pl.BlockSpec((bm, d), lambda i: (i, 0)), pl.BlockSpec((1, d), lambda i: (0, 0))],
        out_specs=pl.BlockSpec((bm, d), lambda i: (i, 0)),
        out_shape=jax.ShapeDtypeStruct((m, d), BF16),
        compiler_params=_params(("parallel",), blk),
        name="rmsnorm",
    )(x, g.reshape(1, d))


def _norm_resid_kernel(y_ref, x_ref, g_ref, o_ref):
    o_ref[...] = x_ref[...] + _rms_scale(y_ref[...]) * g_ref[...]


def _norm_resid(y, x, g, bm=256):
    m, d = x.shape
    blk = 3 * _nbytes((bm, d), F32)
    row = pl.BlockSpec((bm, d), lambda i: (i, 0))
    return pl.pallas_call(
        _norm_resid_kernel,
        grid=(m // bm,),
        in_specs=[row, row, pl.BlockSpec((1, d), lambda i: (0, 0))],
        out_specs=row,
        out_shape=jax.ShapeDtypeStruct((m, d), F32),
        compiler_params=_params(("parallel",), blk),
        name="norm_resid",
    )(y, x, g.reshape(1, d))


def _mm_kernel(a_ref, w_ref, o_ref):
    o_ref[...] = jnp.dot(a_ref[...], w_ref[...], preferred_element_type=F32).astype(o_ref.dtype)


def _matmul(a, w, out_dtype, bm, bn, name):
    m, k = a.shape
    n = w.shape[1]
    bm, bn = _pick_block(bm, m), _pick_block(bn, n)
    blk = _nbytes((bm, k), a.dtype) + _nbytes((k, bn), w.dtype) + _nbytes((bm, bn), out_dtype)
    return pl.pallas_call(
        _mm_kernel,
        grid=(m // bm, n // bn),
        in_specs=[pl.BlockSpec((bm, k), lambda i, j: (i, 0)), pl.BlockSpec((k, bn), lambda i, j: (0, j))],
        out_specs=pl.BlockSpec((bm, bn), lambda i, j: (i, j)),
        out_shape=jax.ShapeDtypeStruct((m, n), out_dtype),
        compiler_params=_params(("parallel", "arbitrary"), blk),
        name=name,
    )(a, w)


def _mm_norm_resid_kernel(a_ref, w_ref, x_ref, g_ref, o_ref):
    j = pl.program_id(1)
    bn = w_ref.shape[1]
    col = pl.multiple_of(j * bn, bn)
    o_ref[:, pl.ds(col, bn)] = jnp.dot(a_ref[...], w_ref[...], preferred_element_type=F32)

    @pl.when(j == pl.num_programs(1) - 1)
    def _():
        o_ref[...] = x_ref[...] + _rms_scale(o_ref[...]) * g_ref[...]


def _matmul_norm_resid(a, w, x, g, bm, bn, name):
    m, k = a.shape
    n = w.shape[1]
    bm, bn = _pick_block(bm, m), _pick_block(bn, n)
    blk = _nbytes((bm, k), a.dtype) + _nbytes((k, bn), w.dtype) + 2 * _nbytes((bm, n), F32)
    return pl.pallas_call(
        _mm_norm_resid_kernel,
        grid=(m // bm, n // bn),
        in_specs=[pl.BlockSpec((bm, k), lambda i, j: (i, 0)), pl.BlockSpec((k, bn), lambda i, j: (0, j)),
                  pl.BlockSpec((bm, n), lambda i, j: (i, 0)), pl.BlockSpec((1, n), lambda i, j: (0, 0))],
        out_specs=pl.BlockSpec((bm, n), lambda i, j: (i, 0)),
        out_shape=jax.ShapeDtypeStruct((m, n), F32),
        compiler_params=_params(("parallel", "arbitrary"), blk),
        name=name,
    )(a, w, x, g.reshape(1, n))


def _fox_bias_kernel(sm_ref, bias_ref, ct_ref):
    blk = LANES
    n_blk = sm_ref.shape[0] // blk
    r_i = lax.broadcasted_iota(jnp.int32, (blk, blk), 0)
    c_i = lax.broadcasted_iota(jnp.int32, (blk, blk), 1)
    tril = (r_i >= c_i).astype(BF16)

    def body(i, carry):
        start = pl.multiple_of(i * blk, blk)
        z = sm_ref[pl.ds(start, blk), :] + bias_ref[...]
        cs = _tril_cumsum(tril, _log_sigmoid(z)) + carry
        ct_ref[pl.ds(start, blk), :] = cs
        return cs[blk - 1:blk, :]

    lax.fori_loop(0, n_blk, body, jnp.zeros((1, blk), F32))


def _fox_bias(small, bias_row, batch, seq):
    tok = pl.BlockSpec((seq, LANES), lambda b: (b, 0))
    return pl.pallas_call(
        _fox_bias_kernel,
        grid=(batch,),
        in_specs=[tok, pl.BlockSpec((1, LANES), lambda b: (0, 0))],
        out_specs=tok,
        out_shape=jax.ShapeDtypeStruct((batch * seq, LANES), F32),
        compiler_params=_params(("parallel",), 2 * _nbytes((seq, LANES), F32)),
        name="fox_bias",
    )(small, bias_row)


def _fox_prep_kernel(q_ref, k_ref, cum_ref, qx_ref, kx_ref, *, inv_scale):
    head = pl.program_id(1)
    rows, dh = q_ref.shape
    lane_c = lax.broadcasted_iota(jnp.int32, cum_ref.shape, 1)
    c = jnp.sum(jnp.where(lane_c == head, cum_ref[...], 0.0), axis=1, keepdims=True) * inv_scale
    hi, mid, lo = (t.astype(F32) for t in _split3(c))
    lane = lax.broadcasted_iota(jnp.int32, (rows, dh), 1)
    eq = jnp.where(lane == 0, hi, jnp.where(lane == 1, mid, jnp.where(lane == 2, lo,
                   jnp.where(lane < FOX_EXTRA, 1.0, 0.0))))
    ek = jnp.where(lane < 3, 1.0, jnp.where(lane == 3, -hi, jnp.where(lane == 4, -mid,
                   jnp.where(lane == 5, -lo, 0.0))))
    qx_ref[:, :dh] = q_ref[...]
    qx_ref[:, dh:] = eq.astype(BF16)
    kx_ref[:, :dh] = k_ref[...]
    kx_ref[:, dh:] = ek.astype(BF16)


def _fox_prep(proj, cum_tok, heads, dh, col_q, col_k, rows=1024):
    t = proj.shape[0]
    rows = _pick_block(rows, t)
    cq0, ck0 = col_q // dh, col_k // dh
    out = jax.ShapeDtypeStruct((t, heads * 2 * dh), BF16)
    blk = 2 * _nbytes((rows, dh), BF16) + _nbytes((rows, LANES), F32) + 2 * _nbytes((rows, 2 * dh), BF16)
    return pl.pallas_call(
        functools.partial(_fox_prep_kernel, inv_scale=dh ** 0.5),
        grid=(t // rows, heads),
        in_specs=[pl.BlockSpec((rows, dh), lambda i, h: (i, cq0 + h)),
                  pl.BlockSpec((rows, dh), lambda i, h: (i, ck0 + h)),
                  pl.BlockSpec((rows, LANES), lambda i, h: (i, 0))],
        out_specs=[pl.BlockSpec((rows, 2 * dh), lambda i, h: (i, h))] * 2,
        out_shape=[out, out],
        compiler_params=_params(("parallel", "arbitrary"), blk),
        name="fox_prep",
    )(proj, proj, cum_tok)


def _fox_kernel(qi_tab, kj_tab, qx_ref, kx_ref, v_ref, o_ref, m_scr, l_scr, acc_scr, *, group, dh):
    step = pl.program_id(2)
    qi = qi_tab[step]
    kj = kj_tab[step]
    bq = qx_ref.shape[0]
    bk = kx_ref.shape[0]
    dx = 2 * dh
    c1 = (dh ** -0.5) * LOG2E

    @pl.when(kj == 0)
    def _():
        m_scr[...] = jnp.full_like(m_scr, NEG_BIG)
        l_scr[...] = jnp.zeros_like(l_scr)
        acc_scr[...] = jnp.zeros_like(acc_scr)

    def update(masked):
        scores = [_dot_nt(kx_ref[:, g * dx:(g + 1) * dx], qx_ref[:, g * dx:(g + 1) * dx])
                  for g in range(group)]
        for g in range(group):
            z = scores[g] * c1
            if masked:
                key = lax.broadcasted_iota(jnp.int32, (bk, bq), 0)
                qry = lax.broadcasted_iota(jnp.int32, (bk, bq), 1)
                z = jnp.where(key <= qry, z, NEG_BIG)
            m_prev = m_scr[g]
            m_new = jnp.maximum(m_prev, jnp.max(z, axis=0, keepdims=True))
            alpha = jnp.exp2(m_prev - m_new)
            p = jnp.exp2(z - m_new)
            l_scr[g] = alpha * l_scr[g] + jnp.sum(p, axis=0, keepdims=True)
            acc_scr[g] = alpha * acc_scr[g] + _dot_tn(v_ref[:, g * dh:(g + 1) * dh], p.astype(BF16))
            m_scr[g] = m_new

    @pl.when(kj != qi)
    def _():
        update(False)

    @pl.when(kj == qi)
    def _():
        update(True)
        for g in range(group):
            o_ref[:, g * dh:(g + 1) * dh] = (acc_scr[g] / l_scr[g]).T.astype(o_ref.dtype)


def _fox_attention(qx, kx, proj, batch, seq, heads, dh, col_v, bq=512):
    group = FOX_GROUP
    nq = seq // bq
    pairs = [(i, j) for i in range(nq) for j in range(i + 1)]
    qi_tab = jnp.asarray([p[0] for p in pairs], jnp.int32)
    kj_tab = jnp.asarray([p[1] for p in pairs], jnp.int32)
    cv0 = col_v // (group * dh)
    dxg = group * 2 * dh

    grid_spec = pltpu.PrefetchScalarGridSpec(
        num_scalar_prefetch=2,
        grid=(batch, heads // group, len(pairs)),
        in_specs=[
            pl.BlockSpec((bq, dxg), lambda b, h, s, qt, kt: (b * nq + qt[s], h)),
            pl.BlockSpec((bq, dxg), lambda b, h, s, qt, kt: (b * nq + kt[s], h)),
            pl.BlockSpec((bq, group * dh), lambda b, h, s, qt, kt: (b * nq + kt[s], cv0 + h)),
        ],
        out_specs=pl.BlockSpec((bq, group * dh), lambda b, h, s, qt, kt: (b * nq + qt[s], h)),
        scratch_shapes=[pltpu.VMEM((group, 1, bq), F32), pltpu.VMEM((group, 1, bq), F32),
                        pltpu.VMEM((group, dh, bq), F32)],
    )
    blk = 2 * _nbytes((bq, dxg), BF16) + 2 * _nbytes((bq, group * dh), BF16) + 6 * _nbytes((bq, bq), F32)
    return pl.pallas_call(
        functools.partial(_fox_kernel, group=group, dh=dh),
        grid_spec=grid_spec,
        out_shape=jax.ShapeDtypeStruct((batch * seq, heads * dh), BF16),
        compiler_params=_params(("parallel", "parallel", "arbitrary"), blk),
        name="fox_attention",
    )(qi_tab, kj_tab, qx, kx, proj)


def _gla_kernel(q_ref, k_ref, v_ref, r_ref, low_ref, wup_ref, bup_ref, gn_ref, o_ref, st_ref, kf_ref, b_ref,
                *, group, dk, dv):
    c, sb = GLA_CHUNK, GLA_SUB
    n_sub = c // sb
    rows_per_step = q_ref.shape[0]
    scale = dk ** -0.5

    @pl.when(pl.program_id(2) == 0)
    def _():
        st_ref[...] = jnp.zeros_like(st_ref)

    r_i = lax.broadcasted_iota(jnp.int32, (c, c), 0)
    c_i = lax.broadcasted_iota(jnp.int32, (c, c), 1)
    tril = (r_i >= c_i).astype(BF16)
    sub_shift = sb.bit_length() - 1
    sub_row = jnp.right_shift(r_i, sub_shift)
    sub_col = jnp.right_shift(c_i, sub_shift)
    lane_sb = lax.broadcasted_iota(jnp.int32, (sb, c), 1)
    row_sb = lax.broadcasted_iota(jnp.int32, (sb, c), 0)

    heads = range(group)
    kcols = [slice(g * dk, (g + 1) * dk) for g in heads]
    vcols = [slice(g * dv, (g + 1) * dv) for g in heads]

    def chunk(ci, carry):
        rows = pl.ds(pl.multiple_of(ci * c, c), c)
        low = low_ref[rows, :].astype(BF16)
        z = [jnp.dot(low, wup_ref[:, kcols[g]], preferred_element_type=F32) + bup_ref[:, kcols[g]] for g in heads]
        b = [_tril_cumsum(tril, _log_sigmoid(z[g]) * (1.0 / GLA_TAU)) for g in heads]
        qf = [q_ref[rows, kcols[g]].astype(F32) * scale for g in heads]
        kf = [k_ref[rows, kcols[g]].astype(F32) for g in heads]
        v = [v_ref[rows, vcols[g]] for g in heads]
        st = [st_ref[g] for g in heads]

        o = [_dot_nt((qf[g] * jnp.exp(b[g])).astype(BF16), st[g].astype(BF16)) for g in heads]

        r = []
        for g in heads:
            kf_ref[g] = kf[g]
            b_ref[g] = b[g]
            ends = [jnp.broadcast_to(b_ref[g, (j + 1) * sb - 1:(j + 1) * sb, :], (c, dk)) for j in range(n_sub)]
            end_of_row = jnp.concatenate([e[:sb] for e in ends], axis=0)
            k_hat = (kf[g] * jnp.exp(end_of_row - b[g])).astype(BF16)
            q_stack = jnp.concatenate(
                [(qf[g] * jnp.exp(jnp.minimum(b[g] - ends[j], 0.0))).astype(BF16) for j in range(n_sub - 1)],
                axis=0)
            r.append(_dot_nt(q_stack, k_hat))

        att = []
        for g in heads:
            a = jnp.zeros((c, c), F32)
            for j in range(n_sub - 1):
                a = jnp.where((sub_col == j) & (sub_row > j), r[g][j * c:(j + 1) * c, :], a)
            diag = []
            for i in range(n_sub):
                q_i = qf[g][i * sb:(i + 1) * sb, :]
                b_i = b[g][i * sb:(i + 1) * sb, :]
                tile = jnp.zeros((sb, c), F32)
                for jj in range(sb):
                    j = i * sb + jj
                    dec = jnp.exp(jnp.minimum(b_i - b_ref[g, j:j + 1, :], 0.0))
                    s = jnp.sum(q_i * kf_ref[g, j:j + 1, :] * dec, axis=-1, keepdims=True)
                    tile = jnp.where(lane_sb == j, jnp.where(row_sb >= jj, s, 0.0), tile)
                diag.append(tile)
            att.append((a + jnp.concatenate(diag, axis=0)).astype(BF16))

        for g in heads:
            b_last = b[g][c - 1:c, :]
            o[g] = o[g] + jnp.dot(att[g], v[g], preferred_element_type=F32)
            k_dec = (kf[g] * jnp.exp(b_last - b[g])).astype(BF16)
            st_ref[g] = st[g] * jnp.exp(b_last) + _dot_tn(v[g], k_dec)

        for g in heads:
            gate = r_ref[rows, vcols[g]].astype(F32)
            o_ref[rows, vcols[g]] = (_rms_scale(o[g]) * gn_ref[...] * (gate * _sigmoid(gate))).astype(o_ref.dtype)
        return carry

    lax.fori_loop(0, rows_per_step // c, chunk, 0)


def _gla(proj, small, wup_pad, b_up, g_norm, batch, seq, heads, dk, dv, col_q, col_k, col_v, col_r):
    group = min(GLA_GROUP, heads)
    ts = min(GLA_ROWS, seq)
    nt = seq // ts
    gk, gv = group * dk, group * dv
    cq0, ck0, cv0, cr0 = col_q // gk, col_k // gk, col_v // gv, col_r // gv
    blk = 2 * _nbytes((ts, gk), BF16) + 3 * _nbytes((ts, gv), BF16) + _nbytes((ts, LANES), F32)
    return pl.pallas_call(
        functools.partial(_gla_kernel, group=group, dk=dk, dv=dv),
        grid=(batch, heads // group, nt),
        in_specs=[
            pl.BlockSpec((ts, gk), lambda b, h, t: (b * nt + t, cq0 + h)),
            pl.BlockSpec((ts, gk), lambda b, h, t: (b * nt + t, ck0 + h)),
            pl.BlockSpec((ts, gv), lambda b, h, t: (b * nt + t, cv0 + h)),
            pl.BlockSpec((ts, gv), lambda b, h, t: (b * nt + t, cr0 + h)),
            pl.BlockSpec((ts, LANES), lambda b, h, t: (b * nt + t, 0)),
            pl.BlockSpec((LANES, gk), lambda b, h, t: (0, h)),
            pl.BlockSpec((1, gk), lambda b, h, t: (0, h)),
            pl.BlockSpec((1, dv), lambda b, h, t: (0, 0)),
        ],
        out_specs=pl.BlockSpec((ts, gv), lambda b, h, t: (b * nt + t, h)),
        out_shape=jax.ShapeDtypeStruct((batch * seq, heads * dv), BF16),
        scratch_shapes=[pltpu.VMEM((group, dv, dk), F32), pltpu.VMEM((group, GLA_CHUNK, dk), F32),
                        pltpu.VMEM((group, GLA_CHUNK, dk), F32)],
        compiler_params=_params(("parallel", "parallel", "arbitrary"), blk),
        name="gla",
    )(proj, proj, proj, proj, small, wup_pad, b_up, g_norm)


def _merge_kernel(a_ref, wa_ref, f_ref, wf_ref, g0_ref, g1_ref, b0_ref, b1_ref, o_ref):
    ya = jnp.dot(a_ref[...], wa_ref[...], preferred_element_type=F32)
    yb = jnp.dot(f_ref[...], wf_ref[...], preferred_element_type=F32)
    g0 = _sigmoid(g0_ref[...].astype(F32) + b0_ref[...])
    g1 = _sigmoid(g1_ref[...].astype(F32) + b1_ref[...])
    o_ref[...] = (g0 * ya + g1 * yb).astype(o_ref.dtype)


def _merge(o_gla, w_a, o_fox, w_f, proj, b_gate, col_gates, bm=1024, bn=512):
    m, ka = o_gla.shape
    kf = o_fox.shape[1]
    d = w_a.shape[1]
    bm, bn = _pick_block(bm, m), _pick_block(bn, d, col_gates)
    g0, nb = col_gates // bn, d // bn
    blk = (_nbytes((bm, ka), BF16) + _nbytes((bm, kf), BF16) + _nbytes((ka, bn), BF16)
           + _nbytes((kf, bn), BF16) + 3 * _nbytes((bm, bn), BF16))
    return pl.pallas_call(
        _merge_kernel,
        grid=(m // bm, nb),
        in_specs=[
            pl.BlockSpec((bm, ka), lambda i, j: (i, 0)),
            pl.BlockSpec((ka, bn), lambda i, j: (0, j)),
            pl.BlockSpec((bm, kf), lambda i, j: (i, 0)),
            pl.BlockSpec((kf, bn), lambda i, j: (0, j)),
            pl.BlockSpec((bm, bn), lambda i, j: (i, g0 + j)),
            pl.BlockSpec((bm, bn), lambda i, j: (i, g0 + nb + j)),
            pl.BlockSpec((1, bn), lambda i, j: (0, j)),
            pl.BlockSpec((1, bn), lambda i, j: (0, nb + j)),
        ],
        out_specs=pl.BlockSpec((bm, bn), lambda i, j: (i, j)),
        out_shape=jax.ShapeDtypeStruct((m, d), BF16),
        compiler_params=_params(("parallel", "arbitrary"), blk),
        name="branch_merge",
    )(o_gla, w_a, o_fox, w_f, proj, proj, b_gate, b_gate)


def _xattn_kernel(q_ref, kv_ref, o_ref, *, heads, dh):
    for h in range(heads):
        q = q_ref[:, h * dh:(h + 1) * dh]
        k = kv_ref[:, h * dh:(h + 1) * dh]
        v = kv_ref[:, (heads + h) * dh:(heads + h + 1) * dh]
        s = _dot_nt(q, k) * (dh ** -0.5)
        p = jnp.exp(s - jnp.max(s, axis=-1, keepdims=True))
        l = jnp.sum(p, axis=-1, keepdims=True)
        o = jnp.dot(p.astype(BF16), v, preferred_element_type=F32) / l
        o_ref[:, h * dh:(h + 1) * dh] = o.astype(o_ref.dtype)


def _xattn(q, kv, batch, seq, mem_len, heads, dh, bq=512):
    nq = seq // bq
    w = heads * dh
    blk = 2 * _nbytes((bq, w), BF16) + _nbytes((mem_len, 2 * w), BF16) + 4 * _nbytes((bq, mem_len), F32)
    return pl.pallas_call(
        functools.partial(_xattn_kernel, heads=heads, dh=dh),
        grid=(batch, nq),
        in_specs=[pl.BlockSpec((bq, w), lambda b, i: (b * nq + i, 0)),
                  pl.BlockSpec((mem_len, 2 * w), lambda b, i: (b, 0))],
        out_specs=pl.BlockSpec((bq, w), lambda b, i: (b * nq + i, 0)),
        out_shape=jax.ShapeDtypeStruct((batch * seq, w), BF16),
        compiler_params=_params(("parallel", "arbitrary"), blk),
        name="cross_attention",
    )(q, kv)


def _ffn_up_kernel(a_ref, wg_ref, wu_ref, cg_ref, cu_ref, bg_ref, bu_ref, o_ref, carry_ref, *, blocks_per_seq):
    i = pl.program_id(1)
    bm = a_ref.shape[0]

    @pl.when(lax.rem(i, blocks_per_seq) == 0)
    def _():
        carry_ref[...] = jnp.zeros_like(carry_ref)

    a = a_ref[...]

    def conv(w_ref, c_ref, b_ref, slot):
        u = jnp.dot(a, w_ref[...], preferred_element_type=F32)
        ext = jnp.concatenate([carry_ref[slot], u], axis=0)
        carry_ref[slot] = u[bm - SUBLANES:, :]
        x1 = pltpu.roll(ext, 1, 0)[SUBLANES:]
        x2 = pltpu.roll(ext, 2, 0)[SUBLANES:]
        return c_ref[0:1, :] * x2 + c_ref[1:2, :] * x1 + c_ref[2:3, :] * u + b_ref[...]

    g = conv(wg_ref, cg_ref, bg_ref, 0)
    u = conv(wu_ref, cu_ref, bu_ref, 1)
    gelu = 0.5 * g * (1.0 + jnp.tanh(GELU_C * (g + 0.044715 * (g * g * g))))
    o_ref[...] = (gelu * u).astype(o_ref.dtype)


def _ffn_up(h, w_up, w_conv, b_conv, seq, d_ff, bm=1024, bn=256):
    m, k = h.shape
    bm, bn = _pick_block(bm, seq), _pick_block(bn, d_ff)
    nb = d_ff // bn
    taps = w_conv.shape[0]
    wsp = lambda off: pl.BlockSpec((k, bn), lambda j, i: (0, off + j))
    csp = lambda off: pl.BlockSpec((taps, bn), lambda j, i: (0, off + j))
    bsp = lambda off: pl.BlockSpec((1, bn), lambda j, i: (0, off + j))
    blk = _nbytes((bm, k), BF16) + 2 * _nbytes((k, bn), BF16) + 8 * _nbytes((bm, bn), F32)
    return pl.pallas_call(
        functools.partial(_ffn_up_kernel, blocks_per_seq=seq // bm),
        grid=(nb, m // bm),
        in_specs=[pl.BlockSpec((bm, k), lambda j, i: (i, 0)), wsp(0), wsp(nb), csp(0), csp(nb), bsp(0), bsp(nb)],
        out_specs=pl.BlockSpec((bm, bn), lambda j, i: (i, j)),
        out_shape=jax.ShapeDtypeStruct((m, d_ff), BF16),
        scratch_shapes=[pltpu.VMEM((2, SUBLANES, bn), F32)],
        compiler_params=_params(("parallel", "arbitrary"), blk),
        name="ffn_up",
    )(h, w_up, w_up, w_conv, w_conv, b_conv, b_conv)


def _layer(x, mem, g_mix_pre, w_in, b_gate, w_gla_gate_up, b_gla_gate, g_gla_norm, b_fox_f,
           w_gla_branch, w_fox_branch, w_out, g_mix_post, g_xa_pre, g_mem, w_xa_q, w_xa_kv,
           w_xa_o, g_xa_post, g_ffn_pre, w_ffn_up, w_conv, b_conv, w_ffn_down, g_ffn_post,
           batch, seq):
    d_model = x.shape[1]
    gla_qk = w_gla_gate_up.shape[1]
    gla_dv = g_gla_norm.shape[0]
    gla_vw = w_gla_branch.shape[0]
    gla_heads = gla_vw // gla_dv
    gla_dk = gla_qk // gla_heads
    fox_w = w_fox_branch.shape[0]
    fox_heads = b_fox_f.shape[0]
    fox_dh = fox_w // fox_heads
    xa_w = w_xa_q.shape[1]
    mem_len = mem.shape[0] // batch
    xa_heads = XA_HEADS
    xa_dh = xa_w // xa_heads
    d_ff = w_ffn_down.shape[0]
    assert fox_heads + GLA_RANK <= LANES and gla_dk == LANES and fox_dh == LANES

    sizes = (gla_qk, gla_qk, gla_vw, gla_vw, GLA_RANK, fox_w, fox_w, fox_w, fox_heads, 2 * d_model)
    offs = [0]
    for s in sizes:
        offs.append(offs[-1] + s)
    o_gq, o_gk, o_gv, o_gr, o_low, o_fq, o_fk, o_fv, o_ff, o_gates = offs[:-1]
    w_big = jnp.concatenate([w_in[:, :o_low], w_in[:, o_fq:o_ff], w_in[:, o_gates:]], axis=1).astype(BF16)
    w_small = jnp.zeros((d_model, LANES), F32)
    w_small = w_small.at[:, :fox_heads].set(w_in[:, o_ff:o_ff + fox_heads])
    w_small = w_small.at[:, fox_heads:fox_heads + GLA_RANK].set(w_in[:, o_low:o_low + GLA_RANK]).astype(BF16)
    c_gq, c_gk, c_gv, c_gr = o_gq, o_gk, o_gv, o_gr
    c_fq = o_low
    c_fk, c_fv = c_fq + fox_w, c_fq + 2 * fox_w
    c_gates = c_fq + 3 * fox_w
    wup_pad = jnp.zeros((LANES, gla_qk), F32).at[fox_heads:fox_heads + GLA_RANK].set(w_gla_gate_up).astype(BF16)
    fox_bias_row = jnp.zeros((1, LANES), F32).at[0, :fox_heads].set(b_fox_f)

    h = _rmsnorm(x, g_mix_pre)
    proj = _matmul(h, w_big, BF16, 1024, 1024, "in_proj")
    small = _matmul(h, w_small, F32, 1024, LANES, "in_proj_small")
    o_gla = _gla(proj, small, wup_pad, b_gla_gate.reshape(1, -1), g_gla_norm.reshape(1, -1),
                 batch, seq, gla_heads, gla_dk, gla_dv, c_gq, c_gk, c_gv, c_gr)
    cum_tok = _fox_bias(small, fox_bias_row, batch, seq)
    qx, kx = _fox_prep(proj, cum_tok, fox_heads, fox_dh, c_fq, c_fk)
    o_fox = _fox_attention(qx, kx, proj, batch, seq, fox_heads, fox_dh, c_fv)
    merged = _merge(o_gla, w_gla_branch.astype(BF16), o_fox, w_fox_branch.astype(BF16), proj,
                    b_gate.reshape(1, -1), c_gates)
    x = _matmul_norm_resid(merged, w_out.astype(BF16), x, g_mix_post, 512, 512, "mix_out")

    h = _rmsnorm(x, g_xa_pre)
    m = _rmsnorm(mem, g_mem)
    q = _matmul(h, w_xa_q.astype(BF16), BF16, 1024, 1024, "xa_q")
    kv = _matmul(m, w_xa_kv.astype(BF16), BF16, 1024, 1024, "xa_kv")
    o = _xattn(q, kv, batch, seq, mem_len, xa_heads, xa_dh)
    x = _matmul_norm_resid(o, w_xa_o.astype(BF16), x, g_xa_post, 512, 512, "xa_out")

    h = _rmsnorm(x, g_ffn_pre)
    act = _ffn_up(h, w_ffn_up.astype(BF16), w_conv, b_conv.reshape(1, -1), seq, d_ff)
    y = _matmul(act, w_ffn_down.astype(BF16), F32, 512, 512, "ffn_down")
    return _norm_resid(y, x, g_ffn_post)


def kernel(x, mem, g_mix_pre, w_in, b_gate, w_gla_gate_up, b_gla_gate, g_gla_norm, b_fox_f, w_gla_branch, w_fox_branch, w_out, g_mix_post, g_xa_pre, g_mem, w_xa_q, w_xa_kv, w_xa_o, g_xa_post, g_ffn_pre, w_ffn_up, w_conv, b_conv, w_ffn_down, g_ffn_post):
    batch, seq, d_model = x.shape
    xf = x.reshape(batch * seq, d_model)
    mf = mem.reshape(-1, d_model)
    per_layer = (g_mix_pre, w_in, b_gate, w_gla_gate_up, b_gla_gate, g_gla_norm, b_fox_f, w_gla_branch,
                 w_fox_branch, w_out, g_mix_post, g_xa_pre, g_mem, w_xa_q, w_xa_kv, w_xa_o, g_xa_post,
                 g_ffn_pre, w_ffn_up, w_conv, b_conv, w_ffn_down, g_ffn_post)
    for l in range(w_in.shape[0]):
        xf = _layer(xf, mf, *(p[l] for p in per_layer), batch=batch, seq=seq)
    return xf.reshape(batch, seq, d_model)
```

```python
import functools
import math

import jax
import jax.numpy as jnp
from jax import lax
from jax.experimental import pallas as pl
from jax.experimental.pallas import tpu as pltpu

F32 = jnp.float32
BF16 = jnp.bfloat16

LANES = 128
SUBLANES = 8
BF16_SUBLANES = 16
V7X_VMEM_BYTES = 64 * 1024 * 1024
VMEM_CAP_BYTES = V7X_VMEM_BYTES - 8 * 1024 * 1024

EPS = 1e-6
GLA_TAU = 16.0
GLA_CHUNK = 64
GLA_SUB = 8
GLA_RANK = 16
GLA_GROUP = 4
GLA_ROWS = 1024
FOX_GROUP = 4
FOX_EXTRA = 6
XA_HEADS = 4
GELU_C = math.sqrt(2.0 / math.pi)
LOG2E = math.log2(math.e)
NEG_BIG = -1e30


def _params(semantics, block_bytes):
    limit = min(VMEM_CAP_BYTES, 2 * block_bytes + 16 * 1024 * 1024)
    return pltpu.CompilerParams(dimension_semantics=semantics, vmem_limit_bytes=int(limit))


def _nbytes(shape, dtype):
    return math.prod(shape) * jnp.dtype(dtype).itemsize


def _pick_block(target, *dims):
    g = math.gcd(*dims)
    best = None
    for b in range(LANES, min(target, g) + 1, LANES):
        if g % b == 0:
            best = b
    assert best is not None, (target, dims)
    return best


def _log_sigmoid(z):
    return jnp.minimum(z, 0.0) - jnp.log(1.0 + jnp.exp(-jnp.abs(z)))


def _sigmoid(z):
    return 1.0 / (1.0 + jnp.exp(-z))


def _split3(x):
    hi = x.astype(BF16)
    r1 = x - hi.astype(F32)
    mid = r1.astype(BF16)
    lo = (r1 - mid.astype(F32)).astype(BF16)
    return hi, mid, lo


def _tril_cumsum(tril, x):
    hi, mid, lo = _split3(x)
    dot = functools.partial(jnp.dot, preferred_element_type=F32)
    return dot(tril, hi) + dot(tril, mid) + dot(tril, lo)


def _dot_nt(a, b):
    return lax.dot_general(a, b, (((1,), (1,)), ((), ())), preferred_element_type=F32)


def _dot_tn(a, b):
    return lax.dot_general(a, b, (((0,), (0,)), ((), ())), preferred_element_type=F32)


def _rms_scale(y):
    return y * lax.rsqrt(jnp.mean(y * y, axis=-1, keepdims=True) + EPS)


def _rmsnorm_kernel(x_ref, g_ref, o_ref):
    o_ref[...] = (_rms_scale(x_ref[...]) * g_ref[...]).astype(o_ref.dtype)


def _rmsnorm(x, g, bm=256):
    m, d = x.shape
    blk = _nbytes((bm, d), F32) + _nbytes((bm, d), BF16)
    return pl.pallas_call(
        _rmsnorm_kernel,
        grid=(m // bm,),
        in_specs=[pl.BlockSpec((bm, d), lambda i: (i, 0)), pl.BlockSpec((1, d), lambda i: (0, 0))],
        out_specs=pl.BlockSpec((bm, d), lambda i: (i, 0)),
        out_shape=jax.ShapeDtypeStruct((m, d), BF16),
        compiler_params=_params(("parallel",), blk),
        name="rmsnorm",
    )(x, g.reshape(1, d))


def _norm_resid_kernel(y_ref, x_ref, g_ref, o_ref):
    o_ref[...] = x_ref[...] + _rms_scale(y_ref[...]) * g_ref[...]


def _norm_resid(y, x, g, bm=256):
    m, d = x.shape
    blk = 3 * _nbytes((bm, d), F32)
    row = pl.BlockSpec((bm, d), lambda i: (i, 0))
    return pl.pallas_call(
        _norm_resid_kernel,
        grid=(m // bm,),
        in_specs=[row, row, pl.BlockSpec((1, d), lambda i: (0, 0))],
        out_specs=row,
        out_shape=jax.ShapeDtypeStruct((m, d), F32),
        compiler_params=_params(("parallel",), blk),
        name="norm_resid",
    )(y, x, g.reshape(1, d))


def _mm_kernel(a_ref, w_ref, o_ref):
    o_ref[...] = jnp.dot(a_ref[...], w_ref[...], preferred_element_type=F32).astype(o_ref.dtype)


def _matmul(a, w, out_dtype, bm, bn, name):
    m, k = a.shape
    n = w.shape[1]
    bm, bn = _pick_block(bm, m), _pick_block(bn, n)
    blk = _nbytes((bm, k), a.dtype) + _nbytes((k, bn), w.dtype) + _nbytes((bm, bn), out_dtype)
    return pl.pallas_call(
        _mm_kernel,
        grid=(m // bm, n // bn),
        in_specs=[pl.BlockSpec((bm, k), lambda i, j: (i, 0)), pl.BlockSpec((k, bn), lambda i, j: (0, j))],
        out_specs=pl.BlockSpec((bm, bn), lambda i, j: (i, j)),
        out_shape=jax.ShapeDtypeStruct((m, n), out_dtype),
        compiler_params=_params(("parallel", "arbitrary"), blk),
        name=name,
    )(a, w)


def _mm_norm_resid_kernel(a_ref, w_ref, x_ref, g_ref, *rest):
    o_ref = rest[-2] if len(rest) == 3 else rest[0]
    j = pl.program_id(1)
    bn = w_ref.shape[1]
    col = pl.multiple_of(j * bn, bn)
    o_ref[:, pl.ds(col, bn)] = jnp.dot(a_ref[...], w_ref[...], preferred_element_type=F32)

    @pl.when(j == pl.num_programs(1) - 1)
    def _():
        y = x_ref[...] + _rms_scale(o_ref[...]) * g_ref[...]
        o_ref[...] = y
        if len(rest) == 3:
            next_gain_ref, _, h_ref = rest
            h_ref[...] = (_rms_scale(y) * next_gain_ref[...]).astype(h_ref.dtype)


def _matmul_norm_resid(a, w, x, g, bm, bn, name, next_gain=None):
    m, k = a.shape
    n = w.shape[1]
    bm, bn = _pick_block(bm, m), _pick_block(bn, n)
    row = pl.BlockSpec((bm, n), lambda i, j: (i, 0))
    vec = pl.BlockSpec((1, n), lambda i, j: (0, 0))
    in_specs = [pl.BlockSpec((bm, k), lambda i, j: (i, 0)), pl.BlockSpec((k, bn), lambda i, j: (0, j)), row, vec]
    args = [a, w, x, g.reshape(1, n)]
    out_specs, out_shape = row, jax.ShapeDtypeStruct((m, n), F32)
    blk = _nbytes((bm, k), a.dtype) + _nbytes((k, bn), w.dtype) + 2 * _nbytes((bm, n), F32)
    if next_gain is not None:
        in_specs.append(vec)
        args.append(next_gain.reshape(1, n))
        out_specs, out_shape = [row, row], [out_shape, jax.ShapeDtypeStruct((m, n), BF16)]
        blk += _nbytes((bm, n), BF16)
    return pl.pallas_call(
        _mm_norm_resid_kernel,
        grid=(m // bm, n // bn),
        in_specs=in_specs,
        out_specs=out_specs,
        out_shape=out_shape,
        compiler_params=_params(("parallel", "arbitrary"), blk),
        name=name,
    )(*args)


def _fox_bias_kernel(sm_ref, bias_ref, ct_ref):
    blk = LANES
    n_blk = sm_ref.shape[0] // blk
    r_i = lax.broadcasted_iota(jnp.int32, (blk, blk), 0)
    c_i = lax.broadcasted_iota(jnp.int32, (blk, blk), 1)
    tril = (r_i >= c_i).astype(BF16)

    def body(i, carry):
        start = pl.multiple_of(i * blk, blk)
        z = sm_ref[pl.ds(start, blk), :] + bias_ref[...]
        cs = _tril_cumsum(tril, _log_sigmoid(z)) + carry
        ct_ref[pl.ds(start, blk), :] = cs
        return cs[blk - 1:blk, :]

    lax.fori_loop(0, n_blk, body, jnp.zeros((1, blk), F32))


def _fox_bias(small, bias_row, batch, seq):
    tok = pl.BlockSpec((seq, LANES), lambda b: (b, 0))
    return pl.pallas_call(
        _fox_bias_kernel,
        grid=(batch,),
        in_specs=[tok, pl.BlockSpec((1, LANES), lambda b: (0, 0))],
        out_specs=tok,
        out_shape=jax.ShapeDtypeStruct((batch * seq, LANES), F32),
        compiler_params=_params(("parallel",), 2 * _nbytes((seq, LANES), F32)),
        name="fox_bias",
    )(small, bias_row)


def _fox_prep_kernel(q_ref, k_ref, v_ref, cum_ref, qx_ref, kx_ref, vt_ref):
    head = pl.program_id(1)
    rows, dh = q_ref.shape
    lane_c = lax.broadcasted_iota(jnp.int32, cum_ref.shape, 1)
    c = jnp.sum(jnp.where(lane_c == head, cum_ref[...], 0.0), axis=1, keepdims=True) * LOG2E
    hi, mid, lo = (t.astype(F32) for t in _split3(c))
    lane = lax.broadcasted_iota(jnp.int32, (rows, dh), 1)
    eq = jnp.where(lane == 0, hi, jnp.where(lane == 1, mid, jnp.where(lane == 2, lo,
                   jnp.where(lane < FOX_EXTRA, 1.0, 0.0))))
    ek = jnp.where(lane < 3, 1.0, jnp.where(lane == 3, -hi, jnp.where(lane == 4, -mid,
                   jnp.where(lane == 5, -lo, 0.0))))
    qx_ref[:, :dh] = q_ref[...]
    qx_ref[:, dh:] = eq.astype(BF16)
    kx_ref[:, :dh] = k_ref[...]
    kx_ref[:, dh:] = ek.astype(BF16)
    vt_ref[:dh, :] = v_ref[...].astype(F32).T.astype(BF16)
    vt_ref[dh:, :] = jnp.ones((vt_ref.shape[0] - dh, rows), BF16)


def _fox_prep(proj, cum_tok, batch, seq, heads, dh, col_q, col_k, col_v, rows=1024):
    t = proj.shape[0]
    rows = _pick_block(rows, seq)
    per_seq = seq // rows
    cq0, ck0, cv0 = col_q // dh, col_k // dh, col_v // dh
    aug = jax.ShapeDtypeStruct((t, heads * 2 * dh), BF16)
    vrows = dh + BF16_SUBLANES
    blk = (3 * _nbytes((rows, dh), BF16) + _nbytes((rows, LANES), F32) + 2 * _nbytes((rows, 2 * dh), BF16)
           + _nbytes((vrows, rows), BF16) + 2 * _nbytes((rows, dh), F32))
    col = lambda c0: pl.BlockSpec((rows, dh), lambda i, h: (i, c0 + h))
    return pl.pallas_call(
        _fox_prep_kernel,
        grid=(t // rows, heads),
        in_specs=[col(cq0), col(ck0), col(cv0), pl.BlockSpec((rows, LANES), lambda i, h: (i, 0))],
        out_specs=[pl.BlockSpec((rows, 2 * dh), lambda i, h: (i, h)),
                   pl.BlockSpec((rows, 2 * dh), lambda i, h: (i, h)),
                   pl.BlockSpec((None, None, vrows, rows), lambda i, h: (i // per_seq, h, 0, i % per_seq))],
        out_shape=[aug, aug, jax.ShapeDtypeStruct((batch, heads, vrows, seq), BF16)],
        compiler_params=_params(("parallel", "arbitrary"), blk),
        name="fox_prep",
    )(proj, proj, proj, cum_tok)


def _fox_kernel(qi_tab, kj_tab, qx_ref, kx_ref, vt_ref, o_ref, m_scr, acc_scr, *, group, dh):
    step = pl.program_id(2)
    qi = qi_tab[step]
    kj = kj_tab[step]
    bq = qx_ref.shape[0]
    bk = kx_ref.shape[0]
    dx = 2 * dh

    @pl.when(kj == 0)
    def _():
        m_scr[...] = jnp.full_like(m_scr, NEG_BIG)
        acc_scr[...] = jnp.zeros_like(acc_scr)

    def update(masked):
        scores = [_dot_nt(kx_ref[:, g * dx:(g + 1) * dx], qx_ref[:, g * dx:(g + 1) * dx])
                  for g in range(group)]
        for g in range(group):
            z = scores[g]
            if masked:
                key = lax.broadcasted_iota(jnp.int32, (bk, bq), 0)
                qry = lax.broadcasted_iota(jnp.int32, (bk, bq), 1)
                z = jnp.where(key <= qry, z, NEG_BIG)
            m_prev = m_scr[g]
            m_new = jnp.maximum(m_prev, jnp.max(z, axis=0, keepdims=True))
            p = jnp.exp2(z - m_new).astype(BF16)
            acc_scr[g] = (jnp.exp2(m_prev - m_new) * acc_scr[g]
                          + jnp.dot(vt_ref[g], p, preferred_element_type=F32))
            m_scr[g] = m_new

    @pl.when(kj != qi)
    def _():
        update(False)

    @pl.when(kj == qi)
    def _():
        update(True)
        for g in range(group):
            acc = acc_scr[g]
            o_ref[:, g * dh:(g + 1) * dh] = (acc[:dh] / acc[dh:dh + 1]).T.astype(o_ref.dtype)


def _fox_attention(qx, kx, vt, batch, seq, heads, dh, bq=512):
    group = min(FOX_GROUP, heads)
    nq = seq // bq
    pairs = [(i, j) for i in range(nq) for j in range(i + 1)]
    qi_tab = jnp.asarray([p[0] for p in pairs], jnp.int32)
    kj_tab = jnp.asarray([p[1] for p in pairs], jnp.int32)
    dxg = group * 2 * dh
    vrows = vt.shape[2]

    grid_spec = pltpu.PrefetchScalarGridSpec(
        num_scalar_prefetch=2,
        grid=(batch, heads // group, len(pairs)),
        in_specs=[
            pl.BlockSpec((bq, dxg), lambda b, h, s, qt, kt: (b * nq + qt[s], h)),
            pl.BlockSpec((bq, dxg), lambda b, h, s, qt, kt: (b * nq + kt[s], h)),
            pl.BlockSpec((None, group, vrows, bq), lambda b, h, s, qt, kt: (b, h, 0, kt[s])),
        ],
        out_specs=pl.BlockSpec((bq, group * dh), lambda b, h, s, qt, kt: (b * nq + qt[s], h)),
        scratch_shapes=[pltpu.VMEM((group, 1, bq), F32), pltpu.VMEM((group, vrows, bq), F32)],
    )
    blk = (2 * _nbytes((bq, dxg), BF16) + _nbytes((group, vrows, bq), BF16) + _nbytes((bq, group * dh), BF16)
           + 6 * _nbytes((bq, bq), F32))
    return pl.pallas_call(
        functools.partial(_fox_kernel, group=group, dh=dh),
        grid_spec=grid_spec,
        out_shape=jax.ShapeDtypeStruct((batch * seq, heads * dh), BF16),
        compiler_params=_params(("parallel", "parallel", "arbitrary"), blk),
        name="fox_attention",
    )(qi_tab, kj_tab, qx, kx, vt)


def _gla_kernel(q_ref, k_ref, v_ref, r_ref, low_ref, wup_ref, bup_ref, gn_ref, o_ref, st_ref, kf_ref, b_ref,
                ball_ref, *, group, dk, dv):
    c, sb = GLA_CHUNK, GLA_SUB
    n_sub = c // sb
    rows_per_step = q_ref.shape[0]
    scale = dk ** -0.5

    @pl.when(pl.program_id(2) == 0)
    def _():
        st_ref[...] = jnp.zeros_like(st_ref)

    r_i = lax.broadcasted_iota(jnp.int32, (c, c), 0)
    c_i = lax.broadcasted_iota(jnp.int32, (c, c), 1)
    tril = (r_i >= c_i).astype(BF16)
    sub_shift = sb.bit_length() - 1
    sub_row = jnp.right_shift(r_i, sub_shift)
    sub_col = jnp.right_shift(c_i, sub_shift)
    lane_sb = lax.broadcasted_iota(jnp.int32, (sb, c), 1)
    row_sb = lax.broadcasted_iota(jnp.int32, (sb, c), 0)

    heads = range(group)
    kcols = [slice(g * dk, (g + 1) * dk) for g in heads]
    vcols = [slice(g * dv, (g + 1) * dv) for g in heads]

    z_all = jnp.dot(low_ref[...].astype(BF16), wup_ref[...], preferred_element_type=F32) + bup_ref[...]
    terms = _split3(_log_sigmoid(z_all) * (LOG2E / GLA_TAU))
    for ci in range(rows_per_step // c):
        rows = slice(ci * c, (ci + 1) * c)
        hi, mid, lo = (jnp.dot(tril, t[rows, :], preferred_element_type=F32) for t in terms)
        ball_ref[rows, :] = hi + mid + lo

    def chunk(ci, carry):
        rows = pl.ds(pl.multiple_of(ci * c, c), c)
        b = [ball_ref[rows, kcols[g]] for g in heads]
        qf = [q_ref[rows, kcols[g]].astype(F32) * scale for g in heads]
        kf = [k_ref[rows, kcols[g]].astype(F32) for g in heads]
        v = [v_ref[rows, vcols[g]] for g in heads]
        st = [st_ref[g] for g in heads]

        o = [_dot_nt((qf[g] * jnp.exp2(b[g])).astype(BF16), st[g].astype(BF16)) for g in heads]

        r = []
        for g in heads:
            kf_ref[g] = kf[g]
            b_ref[g] = b[g]
            ends = [jnp.broadcast_to(b_ref[g, (j + 1) * sb - 1:(j + 1) * sb, :], (c, dk)) for j in range(n_sub)]
            end_of_row = jnp.concatenate([e[:sb] for e in ends], axis=0)
            k_hat = (kf[g] * jnp.exp2(end_of_row - b[g])).astype(BF16)
            q_stack = jnp.concatenate(
                [(qf[g] * jnp.exp2(b[g] - ends[j])).astype(BF16) for j in range(n_sub - 1)],
                axis=0)
            r.append(_dot_nt(q_stack, k_hat))

        att = []
        for g in heads:
            a = jnp.zeros((c, c), F32)
            for j in range(n_sub - 1):
                a = jnp.where((sub_col == j) & (sub_row > j), r[g][j * c:(j + 1) * c, :], a)
            diag = []
            for i in range(n_sub):
                q_i = qf[g][i * sb:(i + 1) * sb, :]
                b_i = b[g][i * sb:(i + 1) * sb, :]
                tile = jnp.zeros((sb, c), F32)
                for j in range(i * sb, (i + 1) * sb):
                    dec = jnp.exp2(b_i - b_ref[g, j:j + 1, :])
                    s = jnp.sum(q_i * kf_ref[g, j:j + 1, :] * dec, axis=-1, keepdims=True)
                    tile = jnp.where(lane_sb == j, s, tile)
                diag.append(jnp.where(lane_sb - i * sb <= row_sb, tile, 0.0))
            att.append((a + jnp.concatenate(diag, axis=0)).astype(BF16))

        for g in heads:
            b_last = b[g][c - 1:c, :]
            o[g] = o[g] + jnp.dot(att[g], v[g], preferred_element_type=F32)
            k_dec = (kf[g] * jnp.exp2(b_last - b[g])).astype(BF16)
            st_ref[g] = st[g] * jnp.exp2(b_last) + _dot_tn(v[g], k_dec)

        for g in heads:
            gate = r_ref[rows, vcols[g]].astype(F32)
            o_ref[rows, vcols[g]] = (_rms_scale(o[g]) * gn_ref[...] * (gate * _sigmoid(gate))).astype(o_ref.dtype)
        return carry

    lax.fori_loop(0, rows_per_step // c, chunk, 0)


def _gla(proj, small, wup_pad, b_up, g_norm, batch, seq, heads, dk, dv, col_q, col_k, col_v, col_r):
    group = min(GLA_GROUP, heads)
    ts = min(GLA_ROWS, seq)
    nt = seq // ts
    gk, gv = group * dk, group * dv
    cq0, ck0, cv0, cr0 = col_q // gk, col_k // gk, col_v // gv, col_r // gv
    blk = 2 * _nbytes((ts, gk), BF16) + 3 * _nbytes((ts, gv), BF16) + _nbytes((ts, LANES), F32)
    return pl.pallas_call(
        functools.partial(_gla_kernel, group=group, dk=dk, dv=dv),
        grid=(batch, heads // group, nt),
        in_specs=[
            pl.BlockSpec((ts, gk), lambda b, h, t: (b * nt + t, cq0 + h)),
            pl.BlockSpec((ts, gk), lambda b, h, t: (b * nt + t, ck0 + h)),
            pl.BlockSpec((ts, gv), lambda b, h, t: (b * nt + t, cv0 + h)),
            pl.BlockSpec((ts, gv), lambda b, h, t: (b * nt + t, cr0 + h)),
            pl.BlockSpec((ts, LANES), lambda b, h, t: (b * nt + t, 0)),
            pl.BlockSpec((LANES, gk), lambda b, h, t: (0, h)),
            pl.BlockSpec((1, gk), lambda b, h, t: (0, h)),
            pl.BlockSpec((1, dv), lambda b, h, t: (0, 0)),
        ],
        out_specs=pl.BlockSpec((ts, gv), lambda b, h, t: (b * nt + t, h)),
        out_shape=jax.ShapeDtypeStruct((batch * seq, heads * dv), BF16),
        scratch_shapes=[pltpu.VMEM((group, dv, dk), F32), pltpu.VMEM((group, GLA_CHUNK, dk), F32),
                        pltpu.VMEM((group, GLA_CHUNK, dk), F32), pltpu.VMEM((ts, gk), F32)],
        compiler_params=_params(("parallel", "parallel", "arbitrary"), blk),
        name="gla",
    )(proj, proj, proj, proj, small, wup_pad, b_up, g_norm)


def _merge_kernel(a_ref, wa_ref, f_ref, wf_ref, g0_ref, g1_ref, b0_ref, b1_ref, o_ref):
    ya = jnp.dot(a_ref[...], wa_ref[...], preferred_element_type=F32)
    yb = jnp.dot(f_ref[...], wf_ref[...], preferred_element_type=F32)
    g0 = _sigmoid(g0_ref[...].astype(F32) + b0_ref[...])
    g1 = _sigmoid(g1_ref[...].astype(F32) + b1_ref[...])
    o_ref[...] = (g0 * ya + g1 * yb).astype(o_ref.dtype)


def _merge(o_gla, w_a, o_fox, w_f, proj, b_gate, col_gates, bm=1024, bn=512):
    m, ka = o_gla.shape
    kf = o_fox.shape[1]
    d = w_a.shape[1]
    bm, bn = _pick_block(bm, m), _pick_block(bn, d, col_gates)
    g0, nb = col_gates // bn, d // bn
    blk = (_nbytes((bm, ka), BF16) + _nbytes((bm, kf), BF16) + _nbytes((ka, bn), BF16)
           + _nbytes((kf, bn), BF16) + 3 * _nbytes((bm, bn), BF16))
    return pl.pallas_call(
        _merge_kernel,
        grid=(m // bm, nb),
        in_specs=[
            pl.BlockSpec((bm, ka), lambda i, j: (i, 0)),
            pl.BlockSpec((ka, bn), lambda i, j: (0, j)),
            pl.BlockSpec((bm, kf), lambda i, j: (i, 0)),
            pl.BlockSpec((kf, bn), lambda i, j: (0, j)),
            pl.BlockSpec((bm, bn), lambda i, j: (i, g0 + j)),
            pl.BlockSpec((bm, bn), lambda i, j: (i, g0 + nb + j)),
            pl.BlockSpec((1, bn), lambda i, j: (0, j)),
            pl.BlockSpec((1, bn), lambda i, j: (0, nb + j)),
        ],
        out_specs=pl.BlockSpec((bm, bn), lambda i, j: (i, j)),
        out_shape=jax.ShapeDtypeStruct((m, d), BF16),
        compiler_params=_params(("parallel", "arbitrary"), blk),
        name="branch_merge",
    )(o_gla, w_a, o_fox, w_f, proj, proj, b_gate, b_gate)


def _xattn_kernel(q_ref, kv_ref, o_ref, *, heads, dh):
    for h in range(heads):
        q = q_ref[:, h * dh:(h + 1) * dh]
        k = kv_ref[:, h * dh:(h + 1) * dh]
        v = kv_ref[:, (heads + h) * dh:(heads + h + 1) * dh]
        s = _dot_nt(q, k) * (dh ** -0.5)
        p = jnp.exp(s - jnp.max(s, axis=-1, keepdims=True))
        l = jnp.sum(p, axis=-1, keepdims=True)
        o = jnp.dot(p.astype(BF16), v, preferred_element_type=F32) / l
        o_ref[:, h * dh:(h + 1) * dh] = o.astype(o_ref.dtype)


def _xattn(q, kv, batch, seq, mem_len, heads, dh, bq=512):
    nq = seq // bq
    w = heads * dh
    blk = 2 * _nbytes((bq, w), BF16) + _nbytes((mem_len, 2 * w), BF16) + 4 * _nbytes((bq, mem_len), F32)
    return pl.pallas_call(
        functools.partial(_xattn_kernel, heads=heads, dh=dh),
        grid=(batch, nq),
        in_specs=[pl.BlockSpec((bq, w), lambda b, i: (b * nq + i, 0)),
                  pl.BlockSpec((mem_len, 2 * w), lambda b, i: (b, 0))],
        out_specs=pl.BlockSpec((bq, w), lambda b, i: (b * nq + i, 0)),
        out_shape=jax.ShapeDtypeStruct((batch * seq, w), BF16),
        compiler_params=_params(("parallel", "arbitrary"), blk),
        name="cross_attention",
    )(q, kv)


def _ffn_up_kernel(a_ref, wg_ref, wu_ref, cg_ref, cu_ref, bg_ref, bu_ref, o_ref, w_scr, carry_ref,
                   *, blocks_per_seq):
    i = pl.program_id(1)
    bm = a_ref.shape[0]

    @pl.when(i == 0)
    def _():
        w_scr[0] = wg_ref[...].astype(BF16)
        w_scr[1] = wu_ref[...].astype(BF16)

    @pl.when(lax.rem(i, blocks_per_seq) == 0)
    def _():
        carry_ref[...] = jnp.zeros_like(carry_ref)

    a = a_ref[...]

    def conv(slot, c_ref, b_ref):
        u = jnp.dot(a, w_scr[slot], preferred_element_type=F32)
        ext = jnp.concatenate([carry_ref[slot], u], axis=0)
        carry_ref[slot] = u[bm - SUBLANES:, :]
        x1 = pltpu.roll(ext, 1, 0)[SUBLANES:]
        x2 = pltpu.roll(ext, 2, 0)[SUBLANES:]
        return c_ref[0:1, :] * x2 + c_ref[1:2, :] * x1 + c_ref[2:3, :] * u + b_ref[...]

    g = conv(0, cg_ref, bg_ref)
    u = conv(1, cu_ref, bu_ref)
    gelu = 0.5 * g * (1.0 + jnp.tanh(GELU_C * (g + 0.044715 * (g * g * g))))
    o_ref[...] = (gelu * u).astype(o_ref.dtype)


def _ffn_up(h, w_up, w_conv, b_conv, seq, d_ff, bm=1024, bn=256):
    m, k = h.shape
    bm, bn = _pick_block(bm, seq), _pick_block(bn, d_ff)
    nb = d_ff // bn
    taps = w_conv.shape[0]
    wsp = lambda off: pl.BlockSpec((k, bn), lambda j, i: (0, off + j))
    csp = lambda off: pl.BlockSpec((taps, bn), lambda j, i: (0, off + j))
    bsp = lambda off: pl.BlockSpec((1, bn), lambda j, i: (0, off + j))
    scratch = 2 * _nbytes((k, bn), BF16) + 8 * _nbytes((bm, bn), F32)
    blk = _nbytes((bm, k), BF16) + 2 * _nbytes((k, bn), F32) + _nbytes((bm, bn), BF16) + scratch // 2
    return pl.pallas_call(
        functools.partial(_ffn_up_kernel, blocks_per_seq=seq // bm),
        grid=(nb, m // bm),
        in_specs=[pl.BlockSpec((bm, k), lambda j, i: (i, 0)), wsp(0), wsp(nb), csp(0), csp(nb), bsp(0), bsp(nb)],
        out_specs=pl.BlockSpec((bm, bn), lambda j, i: (i, j)),
        out_shape=jax.ShapeDtypeStruct((m, d_ff), BF16),
        scratch_shapes=[pltpu.VMEM((2, k, bn), BF16), pltpu.VMEM((2, SUBLANES, bn), F32)],
        compiler_params=_params(("parallel", "arbitrary"), blk),
        name="ffn_up",
    )(h, w_up, w_up, w_conv, w_conv, b_conv, b_conv)


def _layer(x, mem, g_mix_pre, w_in, b_gate, w_gla_gate_up, b_gla_gate, g_gla_norm, b_fox_f,
           w_gla_branch, w_fox_branch, w_out, g_mix_post, g_xa_pre, g_mem, w_xa_q, w_xa_kv,
           w_xa_o, g_xa_post, g_ffn_pre, w_ffn_up, w_conv, b_conv, w_ffn_down, g_ffn_post,
           batch, seq):
    d_model = x.shape[1]
    gla_qk = w_gla_gate_up.shape[1]
    gla_dv = g_gla_norm.shape[0]
    gla_vw = w_gla_branch.shape[0]
    gla_heads = gla_vw // gla_dv
    gla_dk = gla_qk // gla_heads
    fox_w = w_fox_branch.shape[0]
    fox_heads = b_fox_f.shape[0]
    fox_dh = fox_w // fox_heads
    xa_w = w_xa_q.shape[1]
    mem_len = mem.shape[0] // batch
    xa_heads = XA_HEADS
    xa_dh = xa_w // xa_heads
    d_ff = w_ffn_down.shape[0]
    assert fox_heads + GLA_RANK <= LANES and gla_dk == LANES and fox_dh == LANES

    sizes = (gla_qk, gla_qk, gla_vw, gla_vw, GLA_RANK, fox_w, fox_w, fox_w, fox_heads, 2 * d_model)
    offs = [0]
    for s in sizes:
        offs.append(offs[-1] + s)
    o_gq, o_gk, o_gv, o_gr, o_low, o_fq, o_fk, o_fv, o_ff, o_gates = offs[:-1]
    fox_q_scale = LOG2E * fox_dh ** -0.5
    w_big = jnp.concatenate([w_in[:, :o_low], w_in[:, o_fq:o_fk] * fox_q_scale, w_in[:, o_fk:o_ff],
                             w_in[:, o_gates:]], axis=1).astype(BF16)
    w_small = jnp.zeros((d_model, LANES), F32)
    w_small = w_small.at[:, :fox_heads].set(w_in[:, o_ff:o_ff + fox_heads])
    w_small = w_small.at[:, fox_heads:fox_heads + GLA_RANK].set(w_in[:, o_low:o_low + GLA_RANK]).astype(BF16)
    c_gq, c_gk, c_gv, c_gr = o_gq, o_gk, o_gv, o_gr
    c_fq = o_low
    c_fk, c_fv = c_fq + fox_w, c_fq + 2 * fox_w
    c_gates = c_fq + 3 * fox_w
    wup_pad = jnp.zeros((LANES, gla_qk), F32).at[fox_heads:fox_heads + GLA_RANK].set(w_gla_gate_up).astype(BF16)
    fox_bias_row = jnp.zeros((1, LANES), F32).at[0, :fox_heads].set(b_fox_f)

    h = _rmsnorm(x, g_mix_pre)
    proj = _matmul(h, w_big, BF16, 1024, 1024, "in_proj")
    small = _matmul(h, w_small, F32, 1024, LANES, "in_proj_small")
    o_gla = _gla(proj, small, wup_pad, b_gla_gate.reshape(1, -1), g_gla_norm.reshape(1, -1),
                 batch, seq, gla_heads, gla_dk, gla_dv, c_gq, c_gk, c_gv, c_gr)
    cum_tok = _fox_bias(small, fox_bias_row, batch, seq)
    qx, kx, vt = _fox_prep(proj, cum_tok, batch, seq, fox_heads, fox_dh, c_fq, c_fk, c_fv)
    o_fox = _fox_attention(qx, kx, vt, batch, seq, fox_heads, fox_dh)
    merged = _merge(o_gla, w_gla_branch.astype(BF16), o_fox, w_fox_branch.astype(BF16), proj,
                    b_gate.reshape(1, -1), c_gates)
    x = _matmul_norm_resid(merged, w_out.astype(BF16), x, g_mix_post, 512, 512, "mix_out")

    h = _rmsnorm(x, g_xa_pre)
    m = _rmsnorm(mem, g_mem)
    q = _matmul(h, w_xa_q.astype(BF16), BF16, 1024, 1024, "xa_q")
    kv = _matmul(m, w_xa_kv.astype(BF16), BF16, 1024, 1024, "xa_kv")
    o = _xattn(q, kv, batch, seq, mem_len, xa_heads, xa_dh)
    x, h = _matmul_norm_resid(o, w_xa_o.astype(BF16), x, g_xa_post, 256, d_model, "xa_out",
                              next_gain=g_ffn_pre)

    act = _ffn_up(h, w_ffn_up, w_conv, b_conv.reshape(1, -1), seq, d_ff)
    y = _matmul(act, w_ffn_down.astype(BF16), F32, 512, 512, "ffn_down")
    return _norm_resid(y, x, g_ffn_post)


def kernel(x, mem, g_mix_pre, w_in, b_gate, w_gla_gate_up, b_gla_gate, g_gla_norm, b_fox_f, w_gla_branch, w_fox_branch, w_out, g_mix_post, g_xa_pre, g_mem, w_xa_q, w_xa_kv, w_xa_o, g_xa_post, g_ffn_pre, w_ffn_up, w_conv, b_conv, w_ffn_down, g_ffn_post):
    batch, seq, d_model = x.shape
    xf = x.reshape(batch * seq, d_model)
    mf = mem.reshape(-1, d_model)
    per_layer = (g_mix_pre, w_in, b_gate, w_gla_gate_up, b_gla_gate, g_gla_norm, b_fox_f, w_gla_branch,
                 w_fox_branch, w_out, g_mix_post, g_xa_pre, g_mem, w_xa_q, w_xa_kv, w_xa_o, g_xa_post,
                 g_ffn_pre, w_ffn_up, w_conv, b_conv, w_ffn_down, g_ffn_post)
    for l in range(w_in.shape[0]):
        xf = _layer(xf, mf, *(p[l] for p in per_layer), batch=batch, seq=seq)
    return xf.reshape(batch, seq, d_model)
```

```python
import functools
import math

import jax
import jax.numpy as jnp
from jax import lax
from jax.experimental import pallas as pl
from jax.experimental.pallas import tpu as pltpu

F32 = jnp.float32
BF16 = jnp.bfloat16

LANES = 128
SUBLANES = 8
BF16_SUBLANES = 16
V7X_VMEM_BYTES = 64 * 1024 * 1024
VMEM_CAP_BYTES = V7X_VMEM_BYTES - 8 * 1024 * 1024

EPS = 1e-6
GLA_TAU = 16.0
GLA_CHUNK = 64
GLA_SUB = 8
GLA_RANK = 16
GLA_GROUP = 4
GLA_ROWS = 1024
FOX_GROUP = 8
FOX_EXTRA = 6
XA_HEADS = 4
GELU_C = math.sqrt(2.0 / math.pi)
LOG2E = math.log2(math.e)
NEG_BIG = -1e30


def _params(semantics, block_bytes):
    limit = min(VMEM_CAP_BYTES, 2 * block_bytes + 16 * 1024 * 1024)
    return pltpu.CompilerParams(dimension_semantics=semantics, vmem_limit_bytes=int(limit))


def _nbytes(shape, dtype):
    return math.prod(shape) * jnp.dtype(dtype).itemsize


def _pick_block(target, *dims):
    g = math.gcd(*dims)
    best = None
    for b in range(LANES, min(target, g) + 1, LANES):
        if g % b == 0:
            best = b
    assert best is not None, (target, dims)
    return best


def _log_sigmoid(z):
    return jnp.minimum(z, 0.0) - jnp.log(1.0 + jnp.exp(-jnp.abs(z)))


def _sigmoid(z):
    return 1.0 / (1.0 + jnp.exp(-z))


def _split3(x):
    hi = x.astype(BF16)
    r1 = x - hi.astype(F32)
    mid = r1.astype(BF16)
    lo = (r1 - mid.astype(F32)).astype(BF16)
    return hi, mid, lo


def _tril_cumsum(tril, x):
    hi, mid, lo = _split3(x)
    dot = functools.partial(jnp.dot, preferred_element_type=F32)
    return dot(tril, hi) + dot(tril, mid) + dot(tril, lo)


def _dot_nt(a, b):
    return lax.dot_general(a, b, (((1,), (1,)), ((), ())), preferred_element_type=F32)


def _dot_tn(a, b):
    return lax.dot_general(a, b, (((0,), (0,)), ((), ())), preferred_element_type=F32)


def _rms_scale(y):
    return y * lax.rsqrt(jnp.mean(y * y, axis=-1, keepdims=True) + EPS)


def _rmsnorm_kernel(x_ref, g_ref, o_ref):
    o_ref[...] = (_rms_scale(x_ref[...]) * g_ref[...]).astype(o_ref.dtype)


def _rmsnorm(x, g, bm=256):
    m, d = x.shape
    blk = _nbytes((bm, d), F32) + _nbytes((bm, d), BF16)
    return pl.pallas_call(
        _rmsnorm_kernel,
        grid=(m // bm,),
        in_specs=[pl.BlockSpec((bm, d), lambda i: (i, 0)), pl.BlockSpec((1, d), lambda i: (0, 0))],
        out_specs=pl.BlockSpec((bm, d), lambda i: (i, 0)),
        out_shape=jax.ShapeDtypeStruct((m, d), BF16),
        compiler_params=_params(("parallel",), blk),
        name="rmsnorm",
    )(x, g.reshape(1, d))


def _norm_resid_kernel(y_ref, x_ref, g_ref, o_ref):
    o_ref[...] = x_ref[...] + _rms_scale(y_ref[...]) * g_ref[...]


def _norm_resid(y, x, g, bm=256):
    m, d = x.shape
    blk = 3 * _nbytes((bm, d), F32)
    row = pl.BlockSpec((bm, d), lambda i: (i, 0))
    return pl.pallas_call(
        _norm_resid_kernel,
        grid=(m // bm,),
        in_specs=[row, row, pl.BlockSpec((1, d), lambda i: (0, 0))],
        out_specs=row,
        out_shape=jax.ShapeDtypeStruct((m, d), F32),
        compiler_params=_params(("parallel",), blk),
        name="norm_resid",
    )(y, x, g.reshape(1, d))


def _mm_kernel(a_ref, w_ref, o_ref, *, w_is_transposed):
    dot = _dot_nt if w_is_transposed else functools.partial(jnp.dot, preferred_element_type=F32)
    o_ref[...] = dot(a_ref[...], w_ref[...]).astype(o_ref.dtype)


def _matmul(a, w, out_dtype, bm, bn, name, w_is_transposed=False):
    m, k = a.shape
    n = w.shape[0] if w_is_transposed else w.shape[1]
    bm, bn = _pick_block(bm, m), _pick_block(bn, n)
    blk = _nbytes((bm, k), a.dtype) + _nbytes((k, bn), w.dtype) + _nbytes((bm, bn), out_dtype)
    w_spec = pl.BlockSpec((bn, k), lambda i, j: (j, 0)) if w_is_transposed else pl.BlockSpec((k, bn), lambda i, j: (0, j))
    return pl.pallas_call(
        functools.partial(_mm_kernel, w_is_transposed=w_is_transposed),
        grid=(m // bm, n // bn),
        in_specs=[pl.BlockSpec((bm, k), lambda i, j: (i, 0)), w_spec],
        out_specs=pl.BlockSpec((bm, bn), lambda i, j: (i, j)),
        out_shape=jax.ShapeDtypeStruct((m, n), out_dtype),
        compiler_params=_params(("parallel", "arbitrary"), blk),
        name=name,
    )(a, w)


def _norm_mm_kernel(x_ref, g_ref, w_ref, o_ref, h_scr):
    @pl.when(pl.program_id(1) == 0)
    def _():
        h_scr[...] = (_rms_scale(x_ref[...]) * g_ref[...]).astype(h_scr.dtype)

    o_ref[...] = jnp.dot(h_scr[...], w_ref[...], preferred_element_type=F32).astype(o_ref.dtype)


def _norm_matmul(x, g, w, out_dtype, bm, bn, name):
    m, k = x.shape
    n = w.shape[1]
    bm, bn = _pick_block(bm, m), _pick_block(bn, n)
    blk = (_nbytes((bm, k), F32) + _nbytes((k, bn), w.dtype) + _nbytes((bm, bn), out_dtype)
           + _nbytes((bm, k), BF16) // 2)
    return pl.pallas_call(
        _norm_mm_kernel,
        grid=(m // bm, n // bn),
        in_specs=[pl.BlockSpec((bm, k), lambda i, j: (i, 0)), pl.BlockSpec((1, k), lambda i, j: (0, 0)),
                  pl.BlockSpec((k, bn), lambda i, j: (0, j))],
        out_specs=pl.BlockSpec((bm, bn), lambda i, j: (i, j)),
        out_shape=jax.ShapeDtypeStruct((m, n), out_dtype),
        scratch_shapes=[pltpu.VMEM((bm, k), BF16)],
        compiler_params=_params(("parallel", "arbitrary"), blk),
        name=name,
    )(x, g.reshape(1, k), w)


def _mm_norm_resid_kernel(a_ref, w_ref, x_ref, g_ref, *rest):
    o_ref = rest[-2] if len(rest) == 3 else rest[0]
    j = pl.program_id(1)
    bn = w_ref.shape[1]
    col = pl.multiple_of(j * bn, bn)
    o_ref[:, pl.ds(col, bn)] = jnp.dot(a_ref[...], w_ref[...], preferred_element_type=F32)

    @pl.when(j == pl.num_programs(1) - 1)
    def _():
        y = x_ref[...] + _rms_scale(o_ref[...]) * g_ref[...]
        o_ref[...] = y
        if len(rest) == 3:
            next_gain_ref, _, h_ref = rest
            h_ref[...] = (_rms_scale(y) * next_gain_ref[...]).astype(h_ref.dtype)


def _matmul_norm_resid(a, w, x, g, bm, bn, name, next_gain=None):
    m, k = a.shape
    n = w.shape[1]
    bm, bn = _pick_block(bm, m), _pick_block(bn, n)
    row = pl.BlockSpec((bm, n), lambda i, j: (i, 0))
    vec = pl.BlockSpec((1, n), lambda i, j: (0, 0))
    in_specs = [pl.BlockSpec((bm, k), lambda i, j: (i, 0)), pl.BlockSpec((k, bn), lambda i, j: (0, j)), row, vec]
    args = [a, w, x, g.reshape(1, n)]
    out_specs, out_shape = row, jax.ShapeDtypeStruct((m, n), F32)
    blk = _nbytes((bm, k), a.dtype) + _nbytes((k, bn), w.dtype) + 2 * _nbytes((bm, n), F32)
    if next_gain is not None:
        in_specs.append(vec)
        args.append(next_gain.reshape(1, n))
        out_specs, out_shape = [row, row], [out_shape, jax.ShapeDtypeStruct((m, n), BF16)]
        blk += _nbytes((bm, n), BF16)
    return pl.pallas_call(
        _mm_norm_resid_kernel,
        grid=(m // bm, n // bn),
        in_specs=in_specs,
        out_specs=out_specs,
        out_shape=out_shape,
        compiler_params=_params(("parallel", "arbitrary"), blk),
        name=name,
    )(*args)


def _fox_bias_kernel(sm_ref, bias_ref, ct_ref):
    blk = LANES
    n_blk = sm_ref.shape[0] // blk
    r_i = lax.broadcasted_iota(jnp.int32, (blk, blk), 0)
    c_i = lax.broadcasted_iota(jnp.int32, (blk, blk), 1)
    tril = (r_i >= c_i).astype(BF16)

    def body(i, carry):
        start = pl.multiple_of(i * blk, blk)
        z = sm_ref[pl.ds(start, blk), :] + bias_ref[...]
        cs = _tril_cumsum(tril, _log_sigmoid(z)) + carry
        ct_ref[pl.ds(start, blk), :] = cs
        return cs[blk - 1:blk, :]

    lax.fori_loop(0, n_blk, body, jnp.zeros((1, blk), F32))


def _fox_bias(small, bias_row, batch, seq):
    tok = pl.BlockSpec((seq, LANES), lambda b: (b, 0))
    return pl.pallas_call(
        _fox_bias_kernel,
        grid=(batch,),
        in_specs=[tok, pl.BlockSpec((1, LANES), lambda b: (0, 0))],
        out_specs=tok,
        out_shape=jax.ShapeDtypeStruct((batch * seq, LANES), F32),
        compiler_params=_params(("parallel",), 2 * _nbytes((seq, LANES), F32)),
        name="fox_bias",
    )(small, bias_row)


def _fox_prep_kernel(q_ref, k_ref, v_ref, cum_ref, qx_ref, kx_ref, vt_ref):
    head = pl.program_id(1)
    rows, dh = q_ref.shape
    lane_c = lax.broadcasted_iota(jnp.int32, cum_ref.shape, 1)
    c = jnp.sum(jnp.where(lane_c == head, cum_ref[...], 0.0), axis=1, keepdims=True) * LOG2E
    hi, mid, lo = (t.astype(F32) for t in _split3(c))
    lane = lax.broadcasted_iota(jnp.int32, (rows, dh), 1)
    eq = jnp.where(lane == 0, hi, jnp.where(lane == 1, mid, jnp.where(lane == 2, lo,
                   jnp.where(lane < FOX_EXTRA, 1.0, 0.0))))
    ek = jnp.where(lane < 3, 1.0, jnp.where(lane == 3, -hi, jnp.where(lane == 4, -mid,
                   jnp.where(lane == 5, -lo, 0.0))))
    qx_ref[:, :dh] = q_ref[...]
    qx_ref[:, dh:] = eq.astype(BF16)
    kx_ref[:, :dh] = k_ref[...]
    kx_ref[:, dh:] = ek.astype(BF16)
    vt_ref[:dh, :] = v_ref[...].astype(F32).T.astype(BF16)
    vt_ref[dh:, :] = jnp.ones((vt_ref.shape[0] - dh, rows), BF16)


def _fox_prep(proj, cum_tok, batch, seq, heads, dh, col_q, col_k, col_v, rows=1024):
    t = proj.shape[0]
    rows = _pick_block(rows, seq)
    per_seq = seq // rows
    cq0, ck0, cv0 = col_q // dh, col_k // dh, col_v // dh
    aug = jax.ShapeDtypeStruct((t, heads * 2 * dh), BF16)
    vrows = dh + BF16_SUBLANES
    blk = (3 * _nbytes((rows, dh), BF16) + _nbytes((rows, LANES), F32) + 2 * _nbytes((rows, 2 * dh), BF16)
           + _nbytes((vrows, rows), BF16) + 2 * _nbytes((rows, dh), F32))
    col = lambda c0: pl.BlockSpec((rows, dh), lambda i, h: (i, c0 + h))
    return pl.pallas_call(
        _fox_prep_kernel,
        grid=(t // rows, heads),
        in_specs=[col(cq0), col(ck0), col(cv0), pl.BlockSpec((rows, LANES), lambda i, h: (i, 0))],
        out_specs=[pl.BlockSpec((rows, 2 * dh), lambda i, h: (i, h)),
                   pl.BlockSpec((rows, 2 * dh), lambda i, h: (i, h)),
                   pl.BlockSpec((None, None, vrows, rows), lambda i, h: (i // per_seq, h, 0, i % per_seq))],
        out_shape=[aug, aug, jax.ShapeDtypeStruct((batch, heads, vrows, seq), BF16)],
        compiler_params=_params(("parallel", "arbitrary"), blk),
        name="fox_prep",
    )(proj, proj, proj, cum_tok)


def _fox_kernel(qi_tab, kj_tab, qx_ref, kx_ref, vt_ref, o_ref, m_scr, acc_scr, *, group, dh):
    step = pl.program_id(2)
    qi = qi_tab[step]
    kj = kj_tab[step]
    bq = qx_ref.shape[0]
    bk = kx_ref.shape[0]
    dx = 2 * dh

    @pl.when(kj == 0)
    def _():
        m_scr[...] = jnp.full_like(m_scr, NEG_BIG)
        acc_scr[...] = jnp.zeros_like(acc_scr)

    def update(masked):
        scores = [_dot_nt(kx_ref[:, g * dx:(g + 1) * dx], qx_ref[:, g * dx:(g + 1) * dx])
                  for g in range(group)]
        for g in range(group):
            z = scores[g]
            if masked:
                key = lax.broadcasted_iota(jnp.int32, (bk, bq), 0)
                qry = lax.broadcasted_iota(jnp.int32, (bk, bq), 1)
                z = jnp.where(key <= qry, z, NEG_BIG)
            m_prev = m_scr[g]
            m_new = jnp.maximum(m_prev, jnp.max(z, axis=0, keepdims=True))
            p = jnp.exp2(z - m_new).astype(BF16)
            acc_scr[g] = (jnp.exp2(m_prev - m_new) * acc_scr[g]
                          + jnp.dot(vt_ref[g], p, preferred_element_type=F32))
            m_scr[g] = m_new

    @pl.when(kj != qi)
    def _():
        update(False)

    @pl.when(kj == qi)
    def _():
        update(True)
        for g in range(group):
            acc = acc_scr[g]
            o_ref[:, g * dh:(g + 1) * dh] = (acc[:dh] / acc[dh:dh + 1]).T.astype(o_ref.dtype)


def _fox_attention(qx, kx, vt, batch, seq, heads, dh, bq=512):
    group = min(FOX_GROUP, heads)
    nq = seq // bq
    pairs = [(i, j) for i in range(nq) for j in range(i + 1)]
    qi_tab = jnp.asarray([p[0] for p in pairs], jnp.int32)
    kj_tab = jnp.asarray([p[1] for p in pairs], jnp.int32)
    dxg = group * 2 * dh
    vrows = vt.shape[2]

    grid_spec = pltpu.PrefetchScalarGridSpec(
        num_scalar_prefetch=2,
        grid=(batch, heads // group, len(pairs)),
        in_specs=[
            pl.BlockSpec((bq, dxg), lambda b, h, s, qt, kt: (b * nq + qt[s], h)),
            pl.BlockSpec((bq, dxg), lambda b, h, s, qt, kt: (b * nq + kt[s], h)),
            pl.BlockSpec((None, group, vrows, bq), lambda b, h, s, qt, kt: (b, h, 0, kt[s])),
        ],
        out_specs=pl.BlockSpec((bq, group * dh), lambda b, h, s, qt, kt: (b * nq + qt[s], h)),
        scratch_shapes=[pltpu.VMEM((group, 1, bq), F32), pltpu.VMEM((group, vrows, bq), F32)],
    )
    blk = (2 * _nbytes((bq, dxg), BF16) + _nbytes((group, vrows, bq), BF16) + _nbytes((bq, group * dh), BF16)
           + 6 * _nbytes((bq, bq), F32))
    return pl.pallas_call(
        functools.partial(_fox_kernel, group=group, dh=dh),
        grid_spec=grid_spec,
        out_shape=jax.ShapeDtypeStruct((batch * seq, heads * dh), BF16),
        compiler_params=_params(("parallel", "parallel", "arbitrary"), blk),
        name="fox_attention",
    )(qi_tab, kj_tab, qx, kx, vt)


def _gla_kernel(q_ref, k_ref, v_ref, r_ref, low_ref, wup_ref, bup_ref, gn_ref, o_ref, st_ref, kf_ref, b_ref,
                ball_ref, *, group, dk, dv):
    c, sb = GLA_CHUNK, GLA_SUB
    n_sub = c // sb
    rows_per_step = q_ref.shape[0]
    scale = dk ** -0.5

    @pl.when(pl.program_id(2) == 0)
    def _():
        st_ref[...] = jnp.zeros_like(st_ref)

    r_i = lax.broadcasted_iota(jnp.int32, (c, c), 0)
    c_i = lax.broadcasted_iota(jnp.int32, (c, c), 1)
    tril = (r_i >= c_i).astype(BF16)
    sub_shift = sb.bit_length() - 1
    sub_row = jnp.right_shift(r_i, sub_shift)
    sub_col = jnp.right_shift(c_i, sub_shift)
    lane_sb = lax.broadcasted_iota(jnp.int32, (sb, c), 1)
    row_sb = lax.broadcasted_iota(jnp.int32, (sb, c), 0)

    heads = range(group)
    kcols = [slice(g * dk, (g + 1) * dk) for g in heads]
    vcols = [slice(g * dv, (g + 1) * dv) for g in heads]

    z_all = jnp.dot(low_ref[...].astype(BF16), wup_ref[...], preferred_element_type=F32) + bup_ref[...]
    terms = _split3(_log_sigmoid(z_all) * (LOG2E / GLA_TAU))
    for ci in range(rows_per_step // c):
        rows = slice(ci * c, (ci + 1) * c)
        hi, mid, lo = (jnp.dot(tril, t[rows, :], preferred_element_type=F32) for t in terms)
        ball_ref[rows, :] = hi + mid + lo

    def chunk(ci, carry):
        rows = pl.ds(pl.multiple_of(ci * c, c), c)
        b = [ball_ref[rows, kcols[g]] for g in heads]
        qf = [q_ref[rows, kcols[g]].astype(F32) * scale for g in heads]
        kf = [k_ref[rows, kcols[g]].astype(F32) for g in heads]
        v = [v_ref[rows, vcols[g]] for g in heads]
        st = [st_ref[g] for g in heads]

        o = [_dot_nt((qf[g] * jnp.exp2(b[g])).astype(BF16), st[g].astype(BF16)) for g in heads]

        r = []
        for g in heads:
            kf_ref[g] = kf[g]
            b_ref[g] = b[g]
            ends = [jnp.broadcast_to(b_ref[g, (j + 1) * sb - 1:(j + 1) * sb, :], (c, dk)) for j in range(n_sub)]
            end_of_row = jnp.concatenate([e[:sb] for e in ends], axis=0)
            k_hat = (kf[g] * jnp.exp2(end_of_row - b[g])).astype(BF16)
            q_stack = jnp.concatenate(
                [(qf[g] * jnp.exp2(b[g] - ends[j])).astype(BF16) for j in range(n_sub - 1)],
                axis=0)
            r.append(_dot_nt(q_stack, k_hat))

        att = []
        for g in heads:
            a = jnp.zeros((c, c), F32)
            for j in range(n_sub - 1):
                a = jnp.where((sub_col == j) & (sub_row > j), r[g][j * c:(j + 1) * c, :], a)
            diag = []
            for i in range(n_sub):
                q_i = qf[g][i * sb:(i + 1) * sb, :]
                b_i = b[g][i * sb:(i + 1) * sb, :]
                tile = jnp.zeros((sb, c), F32)
                for j in range(i * sb, (i + 1) * sb):
                    dec = jnp.exp2(b_i - b_ref[g, j:j + 1, :])
                    s = jnp.sum(q_i * kf_ref[g, j:j + 1, :] * dec, axis=-1, keepdims=True)
                    tile = jnp.where(lane_sb == j, s, tile)
                diag.append(jnp.where(lane_sb - i * sb <= row_sb, tile, 0.0))
            att.append((a + jnp.concatenate(diag, axis=0)).astype(BF16))

        for g in heads:
            b_last = b[g][c - 1:c, :]
            o[g] = o[g] + jnp.dot(att[g], v[g], preferred_element_type=F32)
            k_dec = (kf[g] * jnp.exp2(b_last - b[g])).astype(BF16)
            st_ref[g] = st[g] * jnp.exp2(b_last) + _dot_tn(v[g], k_dec)

        for g in heads:
            gate = r_ref[rows, vcols[g]].astype(F32)
            o_ref[rows, vcols[g]] = (_rms_scale(o[g]) * gn_ref[...] * (gate * _sigmoid(gate))).astype(o_ref.dtype)
        return carry

    lax.fori_loop(0, rows_per_step // c, chunk, 0)


def _gla(proj, small, wup_pad, b_up, g_norm, batch, seq, heads, dk, dv, col_q, col_k, col_v, col_r):
    group = min(GLA_GROUP, heads)
    ts = min(GLA_ROWS, seq)
    nt = seq // ts
    gk, gv = group * dk, group * dv
    cq0, ck0, cv0, cr0 = col_q // gk, col_k // gk, col_v // gv, col_r // gv
    blk = 2 * _nbytes((ts, gk), BF16) + 3 * _nbytes((ts, gv), BF16) + _nbytes((ts, LANES), F32)
    return pl.pallas_call(
        functools.partial(_gla_kernel, group=group, dk=dk, dv=dv),
        grid=(batch, heads // group, nt),
        in_specs=[
            pl.BlockSpec((ts, gk), lambda b, h, t: (b * nt + t, cq0 + h)),
            pl.BlockSpec((ts, gk), lambda b, h, t: (b * nt + t, ck0 + h)),
            pl.BlockSpec((ts, gv), lambda b, h, t: (b * nt + t, cv0 + h)),
            pl.BlockSpec((ts, gv), lambda b, h, t: (b * nt + t, cr0 + h)),
            pl.BlockSpec((ts, LANES), lambda b, h, t: (b * nt + t, 0)),
            pl.BlockSpec((LANES, gk), lambda b, h, t: (0, h)),
            pl.BlockSpec((1, gk), lambda b, h, t: (0, h)),
            pl.BlockSpec((1, dv), lambda b, h, t: (0, 0)),
        ],
        out_specs=pl.BlockSpec((ts, gv), lambda b, h, t: (b * nt + t, h)),
        out_shape=jax.ShapeDtypeStruct((batch * seq, heads * dv), BF16),
        scratch_shapes=[pltpu.VMEM((group, dv, dk), F32), pltpu.VMEM((group, GLA_CHUNK, dk), F32),
                        pltpu.VMEM((group, GLA_CHUNK, dk), F32), pltpu.VMEM((ts, gk), F32)],
        compiler_params=_params(("parallel", "parallel", "arbitrary"), blk),
        name="gla",
    )(proj, proj, proj, proj, small, wup_pad, b_up, g_norm)


def _merge_kernel(a_ref, wa_ref, f_ref, wf_ref, g0_ref, g1_ref, b0_ref, b1_ref, o_ref):
    ya = jnp.dot(a_ref[...], wa_ref[...], preferred_element_type=F32)
    yb = jnp.dot(f_ref[...], wf_ref[...], preferred_element_type=F32)
    g0 = _sigmoid(g0_ref[...].astype(F32) + b0_ref[...])
    g1 = _sigmoid(g1_ref[...].astype(F32) + b1_ref[...])
    o_ref[...] = (g0 * ya + g1 * yb).astype(o_ref.dtype)


def _merge(o_gla, w_a, o_fox, w_f, proj, b_gate, col_gates, bm=1024, bn=512):
    m, ka = o_gla.shape
    kf = o_fox.shape[1]
    d = w_a.shape[1]
    bm, bn = _pick_block(bm, m), _pick_block(bn, d, col_gates)
    g0, nb = col_gates // bn, d // bn
    blk = (_nbytes((bm, ka), BF16) + _nbytes((bm, kf), BF16) + _nbytes((ka, bn), BF16)
           + _nbytes((kf, bn), BF16) + 3 * _nbytes((bm, bn), BF16))
    return pl.pallas_call(
        _merge_kernel,
        grid=(m // bm, nb),
        in_specs=[
            pl.BlockSpec((bm, ka), lambda i, j: (i, 0)),
            pl.BlockSpec((ka, bn), lambda i, j: (0, j)),
            pl.BlockSpec((bm, kf), lambda i, j: (i, 0)),
            pl.BlockSpec((kf, bn), lambda i, j: (0, j)),
            pl.BlockSpec((bm, bn), lambda i, j: (i, g0 + j)),
            pl.BlockSpec((bm, bn), lambda i, j: (i, g0 + nb + j)),
            pl.BlockSpec((1, bn), lambda i, j: (0, j)),
            pl.BlockSpec((1, bn), lambda i, j: (0, nb + j)),
        ],
        out_specs=pl.BlockSpec((bm, bn), lambda i, j: (i, j)),
        out_shape=jax.ShapeDtypeStruct((m, d), BF16),
        compiler_params=_params(("parallel", "arbitrary"), blk),
        name="branch_merge",
    )(o_gla, w_a, o_fox, w_f, proj, proj, b_gate, b_gate)


def _xattn_kernel(q_ref, kv_ref, o_ref, *, heads, dh):
    for h in range(heads):
        q = q_ref[:, h * dh:(h + 1) * dh]
        k = kv_ref[:, h * dh:(h + 1) * dh]
        v = kv_ref[:, (heads + h) * dh:(heads + h + 1) * dh]
        s = _dot_nt(q, k) * (dh ** -0.5)
        p = jnp.exp(s - jnp.max(s, axis=-1, keepdims=True))
        l = jnp.sum(p, axis=-1, keepdims=True)
        o = jnp.dot(p.astype(BF16), v, preferred_element_type=F32) / l
        o_ref[:, h * dh:(h + 1) * dh] = o.astype(o_ref.dtype)


def _xattn(q, kv, batch, seq, mem_len, heads, dh, bq=512):
    nq = seq // bq
    w = heads * dh
    blk = 2 * _nbytes((bq, w), BF16) + _nbytes((mem_len, 2 * w), BF16) + 4 * _nbytes((bq, mem_len), F32)
    return pl.pallas_call(
        functools.partial(_xattn_kernel, heads=heads, dh=dh),
        grid=(batch, nq),
        in_specs=[pl.BlockSpec((bq, w), lambda b, i: (b * nq + i, 0)),
                  pl.BlockSpec((mem_len, 2 * w), lambda b, i: (b, 0))],
        out_specs=pl.BlockSpec((bq, w), lambda b, i: (b * nq + i, 0)),
        out_shape=jax.ShapeDtypeStruct((batch * seq, w), BF16),
        compiler_params=_params(("parallel", "arbitrary"), blk),
        name="cross_attention",
    )(q, kv)


def _ffn_up_kernel(a_ref, wg_ref, wu_ref, cg_ref, cu_ref, bg_ref, bu_ref, o_ref, w_scr, raw_ref,
                   *, blocks_per_seq):
    i = pl.program_id(1)
    bm = a_ref.shape[0]

    @pl.when(i == 0)
    def _():
        w_scr[0] = wg_ref[...].astype(BF16)
        w_scr[1] = wu_ref[...].astype(BF16)

    @pl.when(lax.rem(i, blocks_per_seq) == 0)
    def _():
        raw_ref[:, :SUBLANES, :] = jnp.zeros((2, SUBLANES, raw_ref.shape[2]), F32)

    a = a_ref[...]

    def conv(slot, c_ref, b_ref):
        raw_ref[slot, SUBLANES:, :] = jnp.dot(a, w_scr[slot], preferred_element_type=F32)
        x0 = raw_ref[slot, SUBLANES:, :]
        x1 = raw_ref[slot, SUBLANES - 1:SUBLANES - 1 + bm, :]
        x2 = raw_ref[slot, SUBLANES - 2:SUBLANES - 2 + bm, :]
        raw_ref[slot, :SUBLANES, :] = raw_ref[slot, bm:, :]
        return c_ref[0:1, :] * x2 + c_ref[1:2, :] * x1 + c_ref[2:3, :] * x0 + b_ref[...]

    g = conv(0, cg_ref, bg_ref)
    u = conv(1, cu_ref, bu_ref)
    gelu = 0.5 * g * (1.0 + jnp.tanh(GELU_C * (g + 0.044715 * (g * g * g))))
    o_ref[...] = (gelu * u).astype(o_ref.dtype)


def _ffn_up(h, w_up, w_conv, b_conv, seq, d_ff, bm=1024, bn=256):
    m, k = h.shape
    bm, bn = _pick_block(bm, seq), _pick_block(bn, d_ff)
    nb = d_ff // bn
    taps = w_conv.shape[0]
    wsp = lambda off: pl.BlockSpec((k, bn), lambda j, i: (0, off + j))
    csp = lambda off: pl.BlockSpec((taps, bn), lambda j, i: (0, off + j))
    bsp = lambda off: pl.BlockSpec((1, bn), lambda j, i: (0, off + j))
    scratch = 2 * _nbytes((k, bn), BF16) + 8 * _nbytes((bm, bn), F32)
    blk = _nbytes((bm, k), BF16) + 2 * _nbytes((k, bn), F32) + _nbytes((bm, bn), BF16) + scratch // 2
    return pl.pallas_call(
        functools.partial(_ffn_up_kernel, blocks_per_seq=seq // bm),
        grid=(nb, m // bm),
        in_specs=[pl.BlockSpec((bm, k), lambda j, i: (i, 0)), wsp(0), wsp(nb), csp(0), csp(nb), bsp(0), bsp(nb)],
        out_specs=pl.BlockSpec((bm, bn), lambda j, i: (i, j)),
        out_shape=jax.ShapeDtypeStruct((m, d_ff), BF16),
        scratch_shapes=[pltpu.VMEM((2, k, bn), BF16), pltpu.VMEM((2, SUBLANES + bm, bn), F32)],
        compiler_params=_params(("parallel", "arbitrary"), blk),
        name="ffn_up",
    )(h, w_up, w_up, w_conv, w_conv, b_conv, b_conv)


def _layer(x, mem, g_mix_pre, w_in, b_gate, w_gla_gate_up, b_gla_gate, g_gla_norm, b_fox_f,
           w_gla_branch, w_fox_branch, w_out, g_mix_post, g_xa_pre, g_mem, w_xa_q, w_xa_kv,
           w_xa_o, g_xa_post, g_ffn_pre, w_ffn_up, w_conv, b_conv, w_ffn_down, g_ffn_post,
           batch, seq):
    d_model = x.shape[1]
    gla_qk = w_gla_gate_up.shape[1]
    gla_dv = g_gla_norm.shape[0]
    gla_vw = w_gla_branch.shape[0]
    gla_heads = gla_vw // gla_dv
    gla_dk = gla_qk // gla_heads
    fox_w = w_fox_branch.shape[0]
    fox_heads = b_fox_f.shape[0]
    fox_dh = fox_w // fox_heads
    xa_w = w_xa_q.shape[1]
    mem_len = mem.shape[0] // batch
    xa_heads = XA_HEADS
    xa_dh = xa_w // xa_heads
    d_ff = w_ffn_down.shape[0]
    assert fox_heads + GLA_RANK <= LANES and gla_dk == LANES and fox_dh == LANES

    sizes = (gla_qk, gla_qk, gla_vw, gla_vw, GLA_RANK, fox_w, fox_w, fox_w, fox_heads, 2 * d_model)
    offs = [0]
    for s in sizes:
        offs.append(offs[-1] + s)
    o_gq, o_gk, o_gv, o_gr, o_low, o_fq, o_fk, o_fv, o_ff, o_gates = offs[:-1]
    fox_q_scale = LOG2E * fox_dh ** -0.5
    wt = w_in.T
    w_big = jnp.concatenate([wt[:o_low], wt[o_fq:o_fk] * fox_q_scale, wt[o_fk:o_ff], wt[o_gates:]],
                            axis=0).astype(BF16)
    w_small = jnp.concatenate([wt[o_ff:o_ff + fox_heads], wt[o_low:o_low + GLA_RANK],
                               jnp.zeros((LANES - fox_heads - GLA_RANK, d_model), F32)], axis=0).astype(BF16)
    c_gq, c_gk, c_gv, c_gr = o_gq, o_gk, o_gv, o_gr
    c_fq = o_low
    c_fk, c_fv = c_fq + fox_w, c_fq + 2 * fox_w
    c_gates = c_fq + 3 * fox_w
    wup_pad = jnp.zeros((LANES, gla_qk), F32).at[fox_heads:fox_heads + GLA_RANK].set(w_gla_gate_up).astype(BF16)
    fox_bias_row = jnp.zeros((1, LANES), F32).at[0, :fox_heads].set(b_fox_f)

    h = _rmsnorm(x, g_mix_pre)
    proj = _matmul(h, w_big, BF16, 1024, 1024, "in_proj", w_is_transposed=True)
    small = _matmul(h, w_small, F32, 1024, LANES, "in_proj_small", w_is_transposed=True)
    o_gla = _gla(proj, small, wup_pad, b_gla_gate.reshape(1, -1), g_gla_norm.reshape(1, -1),
                 batch, seq, gla_heads, gla_dk, gla_dv, c_gq, c_gk, c_gv, c_gr)
    cum_tok = _fox_bias(small, fox_bias_row, batch, seq)
    qx, kx, vt = _fox_prep(proj, cum_tok, batch, seq, fox_heads, fox_dh, c_fq, c_fk, c_fv)
    o_fox = _fox_attention(qx, kx, vt, batch, seq, fox_heads, fox_dh)
    merged = _merge(o_gla, w_gla_branch.astype(BF16), o_fox, w_fox_branch.astype(BF16), proj,
                    b_gate.reshape(1, -1), c_gates)
    x = _matmul_norm_resid(merged, w_out.astype(BF16), x, g_mix_post, 512, 512, "mix_out")

    q = _norm_matmul(x, g_xa_pre, w_xa_q.astype(BF16), BF16, 512, 1024, "xa_q")
    kv = _norm_matmul(mem, g_mem, w_xa_kv.astype(BF16), BF16, 512, 1024, "xa_kv")
    o = _xattn(q, kv, batch, seq, mem_len, xa_heads, xa_dh)
    x, h = _matmul_norm_resid(o, w_xa_o.astype(BF16), x, g_xa_post, 256, d_model, "xa_out",
                              next_gain=g_ffn_pre)

    act = _ffn_up(h, w_ffn_up, w_conv, b_conv.reshape(1, -1), seq, d_ff)
    y = _matmul(act, w_ffn_down.astype(BF16), F32, 512, 512, "ffn_down")
    return _norm_resid(y, x, g_ffn_post)


def kernel(x, mem, g_mix_pre, w_in, b_gate, w_gla_gate_up, b_gla_gate, g_gla_norm, b_fox_f, w_gla_branch, w_fox_branch, w_out, g_mix_post, g_xa_pre, g_mem, w_xa_q, w_xa_kv, w_xa_o, g_xa_post, g_ffn_pre, w_ffn_up, w_conv, b_conv, w_ffn_down, g_ffn_post):
    batch, seq, d_model = x.shape
    xf = x.reshape(batch * seq, d_model)
    mf = mem.reshape(-1, d_model)
    per_layer = (g_mix_pre, w_in, b_gate, w_gla_gate_up, b_gla_gate, g_gla_norm, b_fox_f, w_gla_branch,
                 w_fox_branch, w_out, g_mix_post, g_xa_pre, g_mem, w_xa_q, w_xa_kv, w_xa_o, g_xa_post,
                 g_ffn_pre, w_ffn_up, w_conv, b_conv, w_ffn_down, g_ffn_post)
    for l in range(w_in.shape[0]):
        xf = _layer(xf, mf, *(p[l] for p in per_layer), batch=batch, seq=seq)
    return xf.reshape(batch, seq, d_model)
```

```python
import functools
import math

import jax
import jax.numpy as jnp
from jax import lax
from jax.experimental import pallas as pl
from jax.experimental.pallas import tpu as pltpu

F32 = jnp.float32
BF16 = jnp.bfloat16

LANES = 128
SUBLANES = 8
BF16_SUBLANES = 16
V7X_VMEM_BYTES = 64 * 1024 * 1024
VMEM_CAP_BYTES = V7X_VMEM_BYTES - 8 * 1024 * 1024

EPS = 1e-6
GLA_TAU = 16.0
GLA_CHUNK = 64
GLA_SUB = 8
GLA_RANK = 16
GLA_GROUP = 4
GLA_ROWS = 1024
FOX_GROUP = 8
FOX_PREP_GROUP = 4
FOX_EXTRA = 6
XA_HEADS = 4
GELU_C = math.sqrt(2.0 / math.pi)
LOG2E = math.log2(math.e)
NEG_BIG = -1e30


def _params(semantics, block_bytes):
    limit = min(VMEM_CAP_BYTES, 2 * block_bytes + 16 * 1024 * 1024)
    return pltpu.CompilerParams(dimension_semantics=semantics, vmem_limit_bytes=int(limit))


def _nbytes(shape, dtype):
    return math.prod(shape) * jnp.dtype(dtype).itemsize


def _pick_block(target, *dims):
    g = math.gcd(*dims)
    best = None
    for b in range(LANES, min(target, g) + 1, LANES):
        if g % b == 0:
            best = b
    assert best is not None, (target, dims)
    return best


def _log_sigmoid(z):
    return jnp.minimum(z, 0.0) - jnp.log(1.0 + jnp.exp(-jnp.abs(z)))


def _sigmoid(z):
    return 1.0 / (1.0 + jnp.exp(-z))


def _split3(x):
    hi = x.astype(BF16)
    r1 = x - hi.astype(F32)
    mid = r1.astype(BF16)
    lo = (r1 - mid.astype(F32)).astype(BF16)
    return hi, mid, lo


def _tril_cumsum(tril, x):
    hi, mid, lo = _split3(x)
    dot = functools.partial(jnp.dot, preferred_element_type=F32)
    return dot(tril, hi) + dot(tril, mid) + dot(tril, lo)


def _dot_nt(a, b):
    return lax.dot_general(a, b, (((1,), (1,)), ((), ())), preferred_element_type=F32)


def _dot_tn(a, b):
    return lax.dot_general(a, b, (((0,), (0,)), ((), ())), preferred_element_type=F32)


def _rms_scale(y):
    return y * lax.rsqrt(jnp.mean(y * y, axis=-1, keepdims=True) + EPS)


def _rmsnorm_kernel(x_ref, g_ref, o_ref):
    o_ref[...] = (_rms_scale(x_ref[...]) * g_ref[...]).astype(o_ref.dtype)


def _rmsnorm(x, g, bm=256):
    m, d = x.shape
    blk = _nbytes((bm, d), F32) + _nbytes((bm, d), BF16)
    return pl.pallas_call(
        _rmsnorm_kernel,
        grid=(m // bm,),
        in_specs=[pl.BlockSpec((bm, d), lambda i: (i, 0)), pl.BlockSpec((1, d), lambda i: (0, 0))],
        out_specs=pl.BlockSpec((bm, d), lambda i: (i, 0)),
        out_shape=jax.ShapeDtypeStruct((m, d), BF16),
        compiler_params=_params(("parallel",), blk),
        name="rmsnorm",
    )(x, g.reshape(1, d))


def _norm_resid_kernel(y_ref, x_ref, g_ref, o_ref):
    o_ref[...] = x_ref[...] + _rms_scale(y_ref[...]) * g_ref[...]


def _norm_resid(y, x, g, bm=256):
    m, d = x.shape
    blk = 3 * _nbytes((bm, d), F32)
    row = pl.BlockSpec((bm, d), lambda i: (i, 0))
    return pl.pallas_call(
        _norm_resid_kernel,
        grid=(m // bm,),
        in_specs=[row, row, pl.BlockSpec((1, d), lambda i: (0, 0))],
        out_specs=row,
        out_shape=jax.ShapeDtypeStruct((m, d), F32),
        compiler_params=_params(("parallel",), blk),
        name="norm_resid",
    )(y, x, g.reshape(1, d))


def _mm_kernel(a_ref, w_ref, o_ref, *, w_is_transposed):
    dot = _dot_nt if w_is_transposed else functools.partial(jnp.dot, preferred_element_type=F32)
    o_ref[...] = dot(a_ref[...], w_ref[...]).astype(o_ref.dtype)


def _matmul(a, w, out_dtype, bm, bn, name, w_is_transposed=False):
    m, k = a.shape
    n = w.shape[0] if w_is_transposed else w.shape[1]
    bm, bn = _pick_block(bm, m), _pick_block(bn, n)
    blk = _nbytes((bm, k), a.dtype) + _nbytes((k, bn), w.dtype) + _nbytes((bm, bn), out_dtype)
    w_spec = pl.BlockSpec((bn, k), lambda i, j: (j, 0)) if w_is_transposed else pl.BlockSpec((k, bn), lambda i, j: (0, j))
    return pl.pallas_call(
        functools.partial(_mm_kernel, w_is_transposed=w_is_transposed),
        grid=(m // bm, n // bn),
        in_specs=[pl.BlockSpec((bm, k), lambda i, j: (i, 0)), w_spec],
        out_specs=pl.BlockSpec((bm, bn), lambda i, j: (i, j)),
        out_shape=jax.ShapeDtypeStruct((m, n), out_dtype),
        compiler_params=_params(("parallel", "arbitrary"), blk),
        name=name,
    )(a, w)


def _in_proj_kernel(row_tab, a_ref, w_ref, s_ref, o_ref):
    o_ref[...] = (_dot_nt(a_ref[...], w_ref[...]) * s_ref[...]).astype(o_ref.dtype)


def _in_proj(a, wt, segments, col_scale, bm, bn, name):
    m, k = a.shape
    bm = _pick_block(bm, m)
    bn = _pick_block(bn, *(rows for _, rows in segments))
    n = sum(rows for _, rows in segments)
    firsts = [first_row + r for first_row, rows in segments for r in range(0, rows, bn)]
    assert all(f % BF16_SUBLANES == 0 for f in firsts)
    row_tab = jnp.asarray([f // BF16_SUBLANES for f in firsts], jnp.int32)

    grid_spec = pltpu.PrefetchScalarGridSpec(
        num_scalar_prefetch=1,
        grid=(m // bm, n // bn),
        in_specs=[pl.BlockSpec((bm, k), lambda i, j, tab: (i, 0)),
                  pl.BlockSpec((pl.Element(bn), pl.Element(k)), lambda i, j, tab: (tab[j] * BF16_SUBLANES, 0)),
                  pl.BlockSpec((1, bn), lambda i, j, tab: (0, j))],
        out_specs=pl.BlockSpec((bm, bn), lambda i, j, tab: (i, j)),
    )
    blk = _nbytes((bm, k), a.dtype) + _nbytes((bn, k), wt.dtype) + _nbytes((bm, bn), BF16)
    return pl.pallas_call(
        _in_proj_kernel,
        grid_spec=grid_spec,
        out_shape=jax.ShapeDtypeStruct((m, n), BF16),
        compiler_params=_params(("parallel", "arbitrary"), blk),
        name=name,
    )(row_tab, a, wt, col_scale)


def _norm_mm_kernel(x_ref, g_ref, w_ref, o_ref, h_scr):
    @pl.when(pl.program_id(1) == 0)
    def _():
        h_scr[...] = (_rms_scale(x_ref[...]) * g_ref[...]).astype(h_scr.dtype)

    o_ref[...] = jnp.dot(h_scr[...], w_ref[...], preferred_element_type=F32).astype(o_ref.dtype)


def _norm_matmul(x, g, w, out_dtype, bm, bn, name):
    m, k = x.shape
    n = w.shape[1]
    bm, bn = _pick_block(bm, m), _pick_block(bn, n)
    blk = (_nbytes((bm, k), F32) + _nbytes((k, bn), w.dtype) + _nbytes((bm, bn), out_dtype)
           + _nbytes((bm, k), BF16) // 2)
    return pl.pallas_call(
        _norm_mm_kernel,
        grid=(m // bm, n // bn),
        in_specs=[pl.BlockSpec((bm, k), lambda i, j: (i, 0)), pl.BlockSpec((1, k), lambda i, j: (0, 0)),
                  pl.BlockSpec((k, bn), lambda i, j: (0, j))],
        out_specs=pl.BlockSpec((bm, bn), lambda i, j: (i, j)),
        out_shape=jax.ShapeDtypeStruct((m, n), out_dtype),
        scratch_shapes=[pltpu.VMEM((bm, k), BF16)],
        compiler_params=_params(("parallel", "arbitrary"), blk),
        name=name,
    )(x, g.reshape(1, k), w)


def _mm_norm_resid_kernel(a_ref, w_ref, x_ref, g_ref, *rest):
    o_ref = rest[-2] if len(rest) == 3 else rest[0]
    kk, j = pl.program_id(1), pl.program_id(2)
    bn = w_ref.shape[1]
    col = pl.ds(pl.multiple_of(j * bn, bn), bn)
    part = jnp.dot(a_ref[...], w_ref[...], preferred_element_type=F32)

    @pl.when(kk == 0)
    def _():
        o_ref[:, col] = part

    @pl.when(kk > 0)
    def _():
        o_ref[:, col] += part

    @pl.when(jnp.logical_and(kk == pl.num_programs(1) - 1, j == pl.num_programs(2) - 1))
    def _():
        y = x_ref[...] + _rms_scale(o_ref[...]) * g_ref[...]
        o_ref[...] = y
        if len(rest) == 3:
            next_gain_ref, _, h_ref = rest
            h_ref[...] = (_rms_scale(y) * next_gain_ref[...]).astype(h_ref.dtype)


def _matmul_norm_resid(a, w, x, g, bm, bn, name, next_gain=None, k_splits=1):
    m, k = a.shape
    n = w.shape[1]
    bm, bn = _pick_block(bm, m), _pick_block(bn, n)
    bk = k // k_splits
    assert bk * k_splits == k and (k_splits == 1 or bk % LANES == 0)
    row = pl.BlockSpec((bm, n), lambda i, s, j: (i, 0))
    vec = pl.BlockSpec((1, n), lambda i, s, j: (0, 0))
    resid = row if k_splits == 1 else pl.BlockSpec((bm, n), lambda i, s, j: (i, 0), pipeline_mode=pl.Buffered(1))
    in_specs = [pl.BlockSpec((bm, bk), lambda i, s, j: (i, s)), pl.BlockSpec((bk, bn), lambda i, s, j: (s, j)),
                resid, vec]
    args = [a, w, x, g.reshape(1, n)]
    out_specs, out_shape = row, jax.ShapeDtypeStruct((m, n), F32)
    blk = _nbytes((bm, bk), a.dtype) + _nbytes((bk, bn), w.dtype) + 2 * _nbytes((bm, n), F32)
    if next_gain is not None:
        in_specs.append(vec)
        args.append(next_gain.reshape(1, n))
        out_specs, out_shape = [row, row], [out_shape, jax.ShapeDtypeStruct((m, n), BF16)]
        blk += _nbytes((bm, n), BF16)
    return pl.pallas_call(
        _mm_norm_resid_kernel,
        grid=(m // bm, k_splits, n // bn),
        in_specs=in_specs,
        out_specs=out_specs,
        out_shape=out_shape,
        compiler_params=_params(("parallel", "arbitrary", "arbitrary"), blk),
        name=name,
    )(*args)


def _fox_bias_kernel(sm_ref, bias_ref, ct_ref):
    blk = LANES
    n_blk = sm_ref.shape[0] // blk
    r_i = lax.broadcasted_iota(jnp.int32, (blk, blk), 0)
    c_i = lax.broadcasted_iota(jnp.int32, (blk, blk), 1)
    tril = (r_i >= c_i).astype(BF16)

    def body(i, carry):
        start = pl.multiple_of(i * blk, blk)
        z = sm_ref[pl.ds(start, blk), :] + bias_ref[...]
        cs = _tril_cumsum(tril, _log_sigmoid(z)) + carry
        ct_ref[pl.ds(start, blk), :] = cs
        return cs[blk - 1:blk, :]

    lax.fori_loop(0, n_blk, body, jnp.zeros((1, blk), F32))


def _fox_bias(small, bias_row, batch, seq):
    tok = pl.BlockSpec((seq, LANES), lambda b: (b, 0))
    return pl.pallas_call(
        _fox_bias_kernel,
        grid=(batch,),
        in_specs=[tok, pl.BlockSpec((1, LANES), lambda b: (0, 0))],
        out_specs=tok,
        out_shape=jax.ShapeDtypeStruct((batch * seq, LANES), F32),
        compiler_params=_params(("parallel",), 2 * _nbytes((seq, LANES), F32)),
        name="fox_bias",
    )(small, bias_row)


def _fox_prep_kernel(q_ref, k_ref, v_ref, cum_ref, qx_ref, kx_ref, vt_ref, *, group, dh):
    rows = q_ref.shape[0]
    lane_c = lax.broadcasted_iota(jnp.int32, cum_ref.shape, 1)
    lane = lax.broadcasted_iota(jnp.int32, (rows, dh), 1)
    cum = cum_ref[...]
    for g in range(group):
        head = pl.program_id(1) * group + g
        c = jnp.sum(jnp.where(lane_c == head, cum, 0.0), axis=1, keepdims=True) * LOG2E
        hi, mid, lo = (t.astype(F32) for t in _split3(c))
        eq = jnp.where(lane == 0, hi, jnp.where(lane == 1, mid, jnp.where(lane == 2, lo,
                       jnp.where(lane < FOX_EXTRA, 1.0, 0.0))))
        ek = jnp.where(lane < 3, 1.0, jnp.where(lane == 3, -hi, jnp.where(lane == 4, -mid,
                       jnp.where(lane == 5, -lo, 0.0))))
        src = slice(g * dh, (g + 1) * dh)
        qx_ref[:, 2 * g * dh:(2 * g + 1) * dh] = q_ref[:, src]
        qx_ref[:, (2 * g + 1) * dh:(2 * g + 2) * dh] = eq.astype(BF16)
        kx_ref[:, 2 * g * dh:(2 * g + 1) * dh] = k_ref[:, src]
        kx_ref[:, (2 * g + 1) * dh:(2 * g + 2) * dh] = ek.astype(BF16)
        vt_ref[g, :dh, :] = v_ref[:, src].astype(F32).T.astype(BF16)
        vt_ref[g, dh:, :] = jnp.ones((vt_ref.shape[1] - dh, rows), BF16)


def _fox_prep(proj, cum_tok, batch, seq, heads, dh, col_q, col_k, col_v, rows=1024):
    t = proj.shape[0]
    group = min(FOX_PREP_GROUP, heads)
    rows = _pick_block(rows, seq)
    per_seq = seq // rows
    gd = group * dh
    cq0, ck0, cv0 = col_q // gd, col_k // gd, col_v // gd
    aug = jax.ShapeDtypeStruct((t, heads * 2 * dh), BF16)
    vrows = dh + BF16_SUBLANES
    blk = (3 * _nbytes((rows, gd), BF16) + _nbytes((rows, LANES), F32) + 2 * _nbytes((rows, 2 * gd), BF16)
           + _nbytes((group, vrows, rows), BF16) + 4 * _nbytes((rows, dh), F32))
    col = lambda c0: pl.BlockSpec((rows, gd), lambda i, h: (i, c0 + h))
    return pl.pallas_call(
        functools.partial(_fox_prep_kernel, group=group, dh=dh),
        grid=(t // rows, heads // group),
        in_specs=[col(cq0), col(ck0), col(cv0), pl.BlockSpec((rows, LANES), lambda i, h: (i, 0))],
        out_specs=[pl.BlockSpec((rows, 2 * gd), lambda i, h: (i, h)),
                   pl.BlockSpec((rows, 2 * gd), lambda i, h: (i, h)),
                   pl.BlockSpec((None, group, vrows, rows), lambda i, h: (i // per_seq, h, 0, i % per_seq))],
        out_shape=[aug, aug, jax.ShapeDtypeStruct((batch, heads, vrows, seq), BF16)],
        compiler_params=_params(("parallel", "arbitrary"), blk),
        name="fox_prep",
    )(proj, proj, proj, cum_tok)


def _fox_kernel(qi_tab, kj_tab, qx_ref, kx_ref, vt_ref, o_ref, m_scr, acc_scr, *, group, dh):
    step = pl.program_id(2)
    qi = qi_tab[step]
    kj = kj_tab[step]
    bq = qx_ref.shape[0]
    bk = kx_ref.shape[0]
    dx = 2 * dh

    @pl.when(kj == 0)
    def _():
        m_scr[...] = jnp.full_like(m_scr, NEG_BIG)
        acc_scr[...] = jnp.zeros_like(acc_scr)

    def update(masked):
        scores = [_dot_nt(kx_ref[:, g * dx:(g + 1) * dx], qx_ref[:, g * dx:(g + 1) * dx])
                  for g in range(group)]
        for g in range(group):
            z = scores[g]
            if masked:
                key = lax.broadcasted_iota(jnp.int32, (bk, bq), 0)
                qry = lax.broadcasted_iota(jnp.int32, (bk, bq), 1)
                z = jnp.where(key <= qry, z, NEG_BIG)
            m_prev = m_scr[g]
            m_new = jnp.maximum(m_prev, jnp.max(z, axis=0, keepdims=True))
            p = jnp.exp2(z - m_new).astype(BF16)
            acc_scr[g] = (jnp.exp2(m_prev - m_new) * acc_scr[g]
                          + jnp.dot(vt_ref[g], p, preferred_element_type=F32))
            m_scr[g] = m_new

    @pl.when(kj != qi)
    def _():
        update(False)

    @pl.when(kj == qi)
    def _():
        update(True)
        for g in range(group):
            acc = acc_scr[g]
            o_ref[:, g * dh:(g + 1) * dh] = (acc[:dh] / acc[dh:dh + 1]).T.astype(o_ref.dtype)


def _fox_attention(qx, kx, vt, batch, seq, heads, dh, bq=512):
    group = min(FOX_GROUP, heads)
    nq = seq // bq
    pairs = [(i, j) for i in range(nq) for j in range(i + 1)]
    qi_tab = jnp.asarray([p[0] for p in pairs], jnp.int32)
    kj_tab = jnp.asarray([p[1] for p in pairs], jnp.int32)
    dxg = group * 2 * dh
    vrows = vt.shape[2]

    grid_spec = pltpu.PrefetchScalarGridSpec(
        num_scalar_prefetch=2,
        grid=(batch, heads // group, len(pairs)),
        in_specs=[
            pl.BlockSpec((bq, dxg), lambda b, h, s, qt, kt: (b * nq + qt[s], h)),
            pl.BlockSpec((bq, dxg), lambda b, h, s, qt, kt: (b * nq + kt[s], h)),
            pl.BlockSpec((None, group, vrows, bq), lambda b, h, s, qt, kt: (b, h, 0, kt[s])),
        ],
        out_specs=pl.BlockSpec((bq, group * dh), lambda b, h, s, qt, kt: (b * nq + qt[s], h)),
        scratch_shapes=[pltpu.VMEM((group, 1, bq), F32), pltpu.VMEM((group, vrows, bq), F32)],
    )
    blk = (2 * _nbytes((bq, dxg), BF16) + _nbytes((group, vrows, bq), BF16) + _nbytes((bq, group * dh), BF16)
           + 6 * _nbytes((bq, bq), F32))
    return pl.pallas_call(
        functools.partial(_fox_kernel, group=group, dh=dh),
        grid_spec=grid_spec,
        out_shape=jax.ShapeDtypeStruct((batch * seq, heads * dh), BF16),
        compiler_params=_params(("parallel", "parallel", "arbitrary"), blk),
        name="fox_attention",
    )(qi_tab, kj_tab, qx, kx, vt)


def _gla_kernel(q_ref, k_ref, v_ref, r_ref, low_ref, wup_ref, bup_ref, gn_ref, o_ref, st_ref, kf_ref, b_ref,
                ball_ref, *, group, dk, dv):
    c, sb = GLA_CHUNK, GLA_SUB
    n_sub = c // sb
    rows_per_step = q_ref.shape[0]
    scale = dk ** -0.5

    @pl.when(pl.program_id(2) == 0)
    def _():
        st_ref[...] = jnp.zeros_like(st_ref)

    r_i = lax.broadcasted_iota(jnp.int32, (c, c), 0)
    c_i = lax.broadcasted_iota(jnp.int32, (c, c), 1)
    tril = (r_i >= c_i).astype(BF16)
    sub_shift = sb.bit_length() - 1
    sub_row = jnp.right_shift(r_i, sub_shift)
    sub_col = jnp.right_shift(c_i, sub_shift)
    lane_sb = lax.broadcasted_iota(jnp.int32, (sb, c), 1)
    row_sb = lax.broadcasted_iota(jnp.int32, (sb, c), 0)

    heads = range(group)
    kcols = [slice(g * dk, (g + 1) * dk) for g in heads]
    vcols = [slice(g * dv, (g + 1) * dv) for g in heads]

    z_all = jnp.dot(low_ref[...].astype(BF16), wup_ref[...], preferred_element_type=F32) + bup_ref[...]
    terms = _split3(_log_sigmoid(z_all) * (LOG2E / GLA_TAU))
    for ci in range(rows_per_step // c):
        rows = slice(ci * c, (ci + 1) * c)
        hi, mid, lo = (jnp.dot(tril, t[rows, :], preferred_element_type=F32) for t in terms)
        ball_ref[rows, :] = hi + mid + lo

    def chunk(ci, carry):
        rows = pl.ds(pl.multiple_of(ci * c, c), c)
        b = [ball_ref[rows, kcols[g]] for g in heads]
        qf = [q_ref[rows, kcols[g]].astype(F32) * scale for g in heads]
        kf = [k_ref[rows, kcols[g]].astype(F32) for g in heads]
        v = [v_ref[rows, vcols[g]] for g in heads]
        st = [st_ref[g] for g in heads]

        o = [_dot_nt((qf[g] * jnp.exp2(b[g])).astype(BF16), st[g].astype(BF16)) for g in heads]

        r = []
        for g in heads:
            kf_ref[g] = kf[g]
            b_ref[g] = b[g]
            ends = [jnp.broadcast_to(b_ref[g, (j + 1) * sb - 1:(j + 1) * sb, :], (c, dk)) for j in range(n_sub)]
            end_of_row = jnp.concatenate([e[:sb] for e in ends], axis=0)
            k_hat = (kf[g] * jnp.exp2(end_of_row - b[g])).astype(BF16)
            q_stack = jnp.concatenate(
                [(qf[g] * jnp.exp2(b[g] - ends[j])).astype(BF16) for j in range(n_sub - 1)],
                axis=0)
            r.append(_dot_nt(q_stack, k_hat))

        att = []
        for g in heads:
            a = jnp.zeros((c, c), F32)
            for j in range(n_sub - 1):
                a = jnp.where((sub_col == j) & (sub_row > j), r[g][j * c:(j + 1) * c, :], a)
            diag = []
            for i in range(n_sub):
                q_i = qf[g][i * sb:(i + 1) * sb, :]
                b_i = b[g][i * sb:(i + 1) * sb, :]
                tile = jnp.zeros((sb, c), F32)
                for j in range(i * sb, (i + 1) * sb):
                    dec = jnp.exp2(b_i - b_ref[g, j:j + 1, :])
                    s = jnp.sum(q_i * kf_ref[g, j:j + 1, :] * dec, axis=-1, keepdims=True)
                    tile = jnp.where(lane_sb == j, s, tile)
                diag.append(jnp.where(lane_sb - i * sb <= row_sb, tile, 0.0))
            att.append((a + jnp.concatenate(diag, axis=0)).astype(BF16))

        for g in heads:
            b_last = b[g][c - 1:c, :]
            o[g] = o[g] + jnp.dot(att[g], v[g], preferred_element_type=F32)
            k_dec = (kf[g] * jnp.exp2(b_last - b[g])).astype(BF16)
            st_ref[g] = st[g] * jnp.exp2(b_last) + _dot_tn(v[g], k_dec)

        for g in heads:
            gate = r_ref[rows, vcols[g]].astype(F32)
            o_ref[rows, vcols[g]] = (_rms_scale(o[g]) * gn_ref[...] * (gate * _sigmoid(gate))).astype(o_ref.dtype)
        return carry

    lax.fori_loop(0, rows_per_step // c, chunk, 0)


def _gla(proj, small, wup_pad, b_up, g_norm, batch, seq, heads, dk, dv, col_q, col_k, col_v, col_r):
    group = min(GLA_GROUP, heads)
    ts = min(GLA_ROWS, seq)
    nt = seq // ts
    gk, gv = group * dk, group * dv
    cq0, ck0, cv0, cr0 = col_q // gk, col_k // gk, col_v // gv, col_r // gv
    blk = 2 * _nbytes((ts, gk), BF16) + 3 * _nbytes((ts, gv), BF16) + _nbytes((ts, LANES), F32)
    return pl.pallas_call(
        functools.partial(_gla_kernel, group=group, dk=dk, dv=dv),
        grid=(batch, heads // group, nt),
        in_specs=[
            pl.BlockSpec((ts, gk), lambda b, h, t: (b * nt + t, cq0 + h)),
            pl.BlockSpec((ts, gk), lambda b, h, t: (b * nt + t, ck0 + h)),
            pl.BlockSpec((ts, gv), lambda b, h, t: (b * nt + t, cv0 + h)),
            pl.BlockSpec((ts, gv), lambda b, h, t: (b * nt + t, cr0 + h)),
            pl.BlockSpec((ts, LANES), lambda b, h, t: (b * nt + t, 0)),
            pl.BlockSpec((LANES, gk), lambda b, h, t: (0, h)),
            pl.BlockSpec((1, gk), lambda b, h, t: (0, h)),
            pl.BlockSpec((1, dv), lambda b, h, t: (0, 0)),
        ],
        out_specs=pl.BlockSpec((ts, gv), lambda b, h, t: (b * nt + t, h)),
        out_shape=jax.ShapeDtypeStruct((batch * seq, heads * dv), BF16),
        scratch_shapes=[pltpu.VMEM((group, dv, dk), F32), pltpu.VMEM((group, GLA_CHUNK, dk), F32),
                        pltpu.VMEM((group, GLA_CHUNK, dk), F32), pltpu.VMEM((ts, gk), F32)],
        compiler_params=_params(("parallel", "parallel", "arbitrary"), blk),
        name="gla",
    )(proj, proj, proj, proj, small, wup_pad, b_up, g_norm)


def _merge_kernel(a_ref, wa_ref, f_ref, wf_ref, g0_ref, g1_ref, b0_ref, b1_ref, o_ref):
    ya = jnp.dot(a_ref[...], wa_ref[...], preferred_element_type=F32)
    yb = jnp.dot(f_ref[...], wf_ref[...], preferred_element_type=F32)
    g0 = _sigmoid(g0_ref[...].astype(F32) + b0_ref[...])
    g1 = _sigmoid(g1_ref[...].astype(F32) + b1_ref[...])
    o_ref[...] = (g0 * ya + g1 * yb).astype(o_ref.dtype)


def _merge(o_gla, w_a, o_fox, w_f, proj, b_gate, col_gates, bm=1024, bn=512):
    m, ka = o_gla.shape
    kf = o_fox.shape[1]
    d = w_a.shape[1]
    bm, bn = _pick_block(bm, m), _pick_block(bn, d, col_gates)
    g0, nb = col_gates // bn, d // bn
    blk = (_nbytes((bm, ka), BF16) + _nbytes((bm, kf), BF16) + _nbytes((ka, bn), BF16)
           + _nbytes((kf, bn), BF16) + 3 * _nbytes((bm, bn), BF16))
    return pl.pallas_call(
        _merge_kernel,
        grid=(m // bm, nb),
        in_specs=[
            pl.BlockSpec((bm, ka), lambda i, j: (i, 0)),
            pl.BlockSpec((ka, bn), lambda i, j: (0, j)),
            pl.BlockSpec((bm, kf), lambda i, j: (i, 0)),
            pl.BlockSpec((kf, bn), lambda i, j: (0, j)),
            pl.BlockSpec((bm, bn), lambda i, j: (i, g0 + j)),
            pl.BlockSpec((bm, bn), lambda i, j: (i, g0 + nb + j)),
            pl.BlockSpec((1, bn), lambda i, j: (0, j)),
            pl.BlockSpec((1, bn), lambda i, j: (0, nb + j)),
        ],
        out_specs=pl.BlockSpec((bm, bn), lambda i, j: (i, j)),
        out_shape=jax.ShapeDtypeStruct((m, d), BF16),
        compiler_params=_params(("parallel", "arbitrary"), blk),
        name="branch_merge",
    )(o_gla, w_a, o_fox, w_f, proj, proj, b_gate, b_gate)


def _xattn_kernel(q_ref, kv_ref, o_ref, *, heads, dh):
    for h in range(heads):
        q = q_ref[:, h * dh:(h + 1) * dh]
        k = kv_ref[:, h * dh:(h + 1) * dh]
        v = kv_ref[:, (heads + h) * dh:(heads + h + 1) * dh]
        s = _dot_nt(q, k) * (dh ** -0.5)
        p = jnp.exp(s - jnp.max(s, axis=-1, keepdims=True))
        l = jnp.sum(p, axis=-1, keepdims=True)
        o = jnp.dot(p.astype(BF16), v, preferred_element_type=F32) / l
        o_ref[:, h * dh:(h + 1) * dh] = o.astype(o_ref.dtype)


def _xattn(q, kv, batch, seq, mem_len, heads, dh, bq=512):
    nq = seq // bq
    w = heads * dh
    blk = 2 * _nbytes((bq, w), BF16) + _nbytes((mem_len, 2 * w), BF16) + 4 * _nbytes((bq, mem_len), F32)
    return pl.pallas_call(
        functools.partial(_xattn_kernel, heads=heads, dh=dh),
        grid=(batch, nq),
        in_specs=[pl.BlockSpec((bq, w), lambda b, i: (b * nq + i, 0)),
                  pl.BlockSpec((mem_len, 2 * w), lambda b, i: (b, 0))],
        out_specs=pl.BlockSpec((bq, w), lambda b, i: (b * nq + i, 0)),
        out_shape=jax.ShapeDtypeStruct((batch * seq, w), BF16),
        compiler_params=_params(("parallel", "arbitrary"), blk),
        name="cross_attention",
    )(q, kv)


def _ffn_up_kernel(a_ref, wg_ref, wu_ref, cg_ref, cu_ref, bg_ref, bu_ref, o_ref, w_scr, raw_ref,
                   *, blocks_per_seq):
    i = pl.program_id(1)
    bm = a_ref.shape[0]

    @pl.when(i == 0)
    def _():
        w_scr[0] = wg_ref[...].astype(BF16)
        w_scr[1] = wu_ref[...].astype(BF16)

    @pl.when(lax.rem(i, blocks_per_seq) == 0)
    def _():
        raw_ref[:, :SUBLANES, :] = jnp.zeros((2, SUBLANES, raw_ref.shape[2]), F32)

    a = a_ref[...]

    def conv(slot, c_ref, b_ref):
        raw_ref[slot, SUBLANES:, :] = jnp.dot(a, w_scr[slot], preferred_element_type=F32)
        x0 = raw_ref[slot, SUBLANES:, :]
        x1 = raw_ref[slot, SUBLANES - 1:SUBLANES - 1 + bm, :]
        x2 = raw_ref[slot, SUBLANES - 2:SUBLANES - 2 + bm, :]
        raw_ref[slot, :SUBLANES, :] = raw_ref[slot, bm:, :]
        return c_ref[0:1, :] * x2 + c_ref[1:2, :] * x1 + c_ref[2:3, :] * x0 + b_ref[...]

    g = conv(0, cg_ref, bg_ref)
    u = conv(1, cu_ref, bu_ref)
    gelu = 0.5 * g * (1.0 + jnp.tanh(GELU_C * (g + 0.044715 * (g * g * g))))
    o_ref[...] = (gelu * u).astype(o_ref.dtype)


def _ffn_up(h, w_up, w_conv, b_conv, seq, d_ff, bm=1024, bn=256):
    m, k = h.shape
    bm, bn = _pick_block(bm, seq), _pick_block(bn, d_ff)
    nb = d_ff // bn
    taps = w_conv.shape[0]
    wsp = lambda off: pl.BlockSpec((k, bn), lambda j, i: (0, off + j))
    csp = lambda off: pl.BlockSpec((taps, bn), lambda j, i: (0, off + j))
    bsp = lambda off: pl.BlockSpec((1, bn), lambda j, i: (0, off + j))
    scratch = 2 * _nbytes((k, bn), BF16) + 8 * _nbytes((bm, bn), F32)
    blk = _nbytes((bm, k), BF16) + 2 * _nbytes((k, bn), F32) + _nbytes((bm, bn), BF16) + scratch // 2
    return pl.pallas_call(
        functools.partial(_ffn_up_kernel, blocks_per_seq=seq // bm),
        grid=(nb, m // bm),
        in_specs=[pl.BlockSpec((bm, k), lambda j, i: (i, 0)), wsp(0), wsp(nb), csp(0), csp(nb), bsp(0), bsp(nb)],
        out_specs=pl.BlockSpec((bm, bn), lambda j, i: (i, j)),
        out_shape=jax.ShapeDtypeStruct((m, d_ff), BF16),
        scratch_shapes=[pltpu.VMEM((2, k, bn), BF16), pltpu.VMEM((2, SUBLANES + bm, bn), F32)],
        compiler_params=_params(("parallel", "arbitrary"), blk),
        name="ffn_up",
    )(h, w_up, w_up, w_conv, w_conv, b_conv, b_conv)


def _layer(x, mem, g_mix_pre, w_in, b_gate, w_gla_gate_up, b_gla_gate, g_gla_norm, b_fox_f,
           w_gla_branch, w_fox_branch, w_out, g_mix_post, g_xa_pre, g_mem, w_xa_q, w_xa_kv,
           w_xa_o, g_xa_post, g_ffn_pre, w_ffn_up, w_conv, b_conv, w_ffn_down, g_ffn_post,
           batch, seq):
    d_model = x.shape[1]
    gla_qk = w_gla_gate_up.shape[1]
    gla_dv = g_gla_norm.shape[0]
    gla_vw = w_gla_branch.shape[0]
    gla_heads = gla_vw // gla_dv
    gla_dk = gla_qk // gla_heads
    fox_w = w_fox_branch.shape[0]
    fox_heads = b_fox_f.shape[0]
    fox_dh = fox_w // fox_heads
    xa_w = w_xa_q.shape[1]
    mem_len = mem.shape[0] // batch
    xa_heads = XA_HEADS
    xa_dh = xa_w // xa_heads
    d_ff = w_ffn_down.shape[0]
    assert fox_heads + GLA_RANK <= LANES and gla_dk == LANES and fox_dh == LANES

    sizes = (gla_qk, gla_qk, gla_vw, gla_vw, GLA_RANK, fox_w, fox_w, fox_w, fox_heads, 2 * d_model)
    offs = [0]
    for s in sizes:
        offs.append(offs[-1] + s)
    o_gq, o_gk, o_gv, o_gr, o_low, o_fq, o_fk, o_fv, o_ff, o_gates = offs[:-1]
    wt = w_in.T.astype(BF16)
    wide_segments = [(o_gq, o_low - o_gq), (o_fq, o_ff - o_fq), (o_gates, 2 * d_model)]
    w_small = jnp.concatenate([wt[o_ff:o_ff + fox_heads], wt[o_low:o_low + GLA_RANK],
                               jnp.zeros((LANES - fox_heads - GLA_RANK, d_model), BF16)], axis=0)
    c_gq, c_gk, c_gv, c_gr = o_gq, o_gk, o_gv, o_gr
    c_fq = o_low
    c_fk, c_fv = c_fq + fox_w, c_fq + 2 * fox_w
    c_gates = c_fq + 3 * fox_w
    col_scale = jnp.ones((1, c_gates + 2 * d_model), F32).at[:, c_fq:c_fk].set(LOG2E * fox_dh ** -0.5)
    wup_pad = jnp.zeros((LANES, gla_qk), F32).at[fox_heads:fox_heads + GLA_RANK].set(w_gla_gate_up).astype(BF16)
    fox_bias_row = jnp.zeros((1, LANES), F32).at[0, :fox_heads].set(b_fox_f)

    h = _rmsnorm(x, g_mix_pre)
    proj = _in_proj(h, wt, wide_segments, col_scale, 1024, 1024, "in_proj")
    small = _matmul(h, w_small, F32, 1024, LANES, "in_proj_small", w_is_transposed=True)
    o_gla = _gla(proj, small, wup_pad, b_gla_gate.reshape(1, -1), g_gla_norm.reshape(1, -1),
                 batch, seq, gla_heads, gla_dk, gla_dv, c_gq, c_gk, c_gv, c_gr)
    cum_tok = _fox_bias(small, fox_bias_row, batch, seq)
    qx, kx, vt = _fox_prep(proj, cum_tok, batch, seq, fox_heads, fox_dh, c_fq, c_fk, c_fv)
    o_fox = _fox_attention(qx, kx, vt, batch, seq, fox_heads, fox_dh)
    merged = _merge(o_gla, w_gla_branch.astype(BF16), o_fox, w_fox_branch.astype(BF16), proj,
                    b_gate.reshape(1, -1), c_gates)
    x = _matmul_norm_resid(merged, w_out.astype(BF16), x, g_mix_post, 512, 512, "mix_out")

    q = _norm_matmul(x, g_xa_pre, w_xa_q.astype(BF16), BF16, 512, 1024, "xa_q")
    kv = _norm_matmul(mem, g_mem, w_xa_kv.astype(BF16), BF16, 512, 1024, "xa_kv")
    o = _xattn(q, kv, batch, seq, mem_len, xa_heads, xa_dh)
    x, h = _matmul_norm_resid(o, w_xa_o.astype(BF16), x, g_xa_post, 256, d_model, "xa_out",
                              next_gain=g_ffn_pre)

    act = _ffn_up(h, w_ffn_up, w_conv, b_conv.reshape(1, -1), seq, d_ff)
    return _matmul_norm_resid(act, w_ffn_down.astype(BF16), x, g_ffn_post, 512, 512, "ffn_down", k_splits=2)


def kernel(x, mem, g_mix_pre, w_in, b_gate, w_gla_gate_up, b_gla_gate, g_gla_norm, b_fox_f, w_gla_branch, w_fox_branch, w_out, g_mix_post, g_xa_pre, g_mem, w_xa_q, w_xa_kv, w_xa_o, g_xa_post, g_ffn_pre, w_ffn_up, w_conv, b_conv, w_ffn_down, g_ffn_post):
    batch, seq, d_model = x.shape
    xf = x.reshape(batch * seq, d_model)
    mf = mem.reshape(-1, d_model)
    per_layer = (g_mix_pre, w_in, b_gate, w_gla_gate_up, b_gla_gate, g_gla_norm, b_fox_f, w_gla_branch,
                 w_fox_branch, w_out, g_mix_post, g_xa_pre, g_mem, w_xa_q, w_xa_kv, w_xa_o, g_xa_post,
                 g_ffn_pre, w_ffn_up, w_conv, b_conv, w_ffn_down, g_ffn_post)
    for l in range(w_in.shape[0]):
        xf = _layer(xf, mf, *(p[l] for p in per_layer), batch=batch, seq=seq)
    return xf.reshape(batch, seq, d_model)
```

```python
import functools
import math

import jax
import jax.numpy as jnp
from jax import lax
from jax.experimental import pallas as pl
from jax.experimental.pallas import tpu as pltpu

F32 = jnp.float32
BF16 = jnp.bfloat16

LANES = 128
SUBLANES = 8
BF16_SUBLANES = 16
V7X_VMEM_BYTES = 64 * 1024 * 1024
VMEM_CAP_BYTES = V7X_VMEM_BYTES - 8 * 1024 * 1024

EPS = 1e-6
GLA_TAU = 16.0
GLA_CHUNK = 64
GLA_SUB = 8
GLA_RANK = 16
GLA_GROUP = 4
GLA_ROWS = 1024
FOX_GROUP = 8
FOX_PREP_GROUP = 4
FOX_EXTRA = 6
XA_HEADS = 4
GELU_C = math.sqrt(2.0 / math.pi)
LOG2E = math.log2(math.e)
NEG_BIG = -1e30


def _params(semantics, block_bytes):
    limit = min(VMEM_CAP_BYTES, 2 * block_bytes + 16 * 1024 * 1024)
    return pltpu.CompilerParams(dimension_semantics=semantics, vmem_limit_bytes=int(limit))


def _nbytes(shape, dtype):
    return math.prod(shape) * jnp.dtype(dtype).itemsize


def _pick_block(target, *dims):
    g = math.gcd(*dims)
    best = None
    for b in range(LANES, min(target, g) + 1, LANES):
        if g % b == 0:
            best = b
    assert best is not None, (target, dims)
    return best


def _log_sigmoid(z):
    return jnp.minimum(z, 0.0) - jnp.log(1.0 + jnp.exp(-jnp.abs(z)))


def _sigmoid(z):
    return 1.0 / (1.0 + jnp.exp(-z))


def _split3(x):
    hi = x.astype(BF16)
    r1 = x - hi.astype(F32)
    mid = r1.astype(BF16)
    lo = (r1 - mid.astype(F32)).astype(BF16)
    return hi, mid, lo


def _tril_cumsum(tril, x):
    hi, mid, lo = _split3(x)
    dot = functools.partial(jnp.dot, preferred_element_type=F32)
    return dot(tril, hi) + dot(tril, mid) + dot(tril, lo)


def _dot_nt(a, b):
    return lax.dot_general(a, b, (((1,), (1,)), ((), ())), preferred_element_type=F32)


def _dot_tn(a, b):
    return lax.dot_general(a, b, (((0,), (0,)), ((), ())), preferred_element_type=F32)


def _rms_scale(y):
    return y * lax.rsqrt(jnp.mean(y * y, axis=-1, keepdims=True) + EPS)


def _rmsnorm_kernel(x_ref, g_ref, o_ref):
    o_ref[...] = (_rms_scale(x_ref[...]) * g_ref[...]).astype(o_ref.dtype)


def _rmsnorm(x, g, bm=256):
    m, d = x.shape
    blk = _nbytes((bm, d), F32) + _nbytes((bm, d), BF16)
    return pl.pallas_call(
        _rmsnorm_kernel,
        grid=(m // bm,),
        in_specs=[pl.BlockSpec((bm, d), lambda i: (i, 0)), pl.BlockSpec((1, d), lambda i: (0, 0))],
        out_specs=pl.BlockSpec((bm, d), lambda i: (i, 0)),
        out_shape=jax.ShapeDtypeStruct((m, d), BF16),
        compiler_params=_params(("parallel",), blk),
        name="rmsnorm",
    )(x, g.reshape(1, d))


def _norm_resid_kernel(y_ref, x_ref, g_ref, o_ref):
    o_ref[...] = x_ref[...] + _rms_scale(y_ref[...]) * g_ref[...]


def _norm_resid(y, x, g, bm=256):
    m, d = x.shape
    blk = 3 * _nbytes((bm, d), F32)
    row = pl.BlockSpec((bm, d), lambda i: (i, 0))
    return pl.pallas_call(
        _norm_resid_kernel,
        grid=(m // bm,),
        in_specs=[row, row, pl.BlockSpec((1, d), lambda i: (0, 0))],
        out_specs=row,
        out_shape=jax.ShapeDtypeStruct((m, d), F32),
        compiler_params=_params(("parallel",), blk),
        name="norm_resid",
    )(y, x, g.reshape(1, d))


def _mm_kernel(a_ref, w_ref, o_ref, *, w_is_transposed):
    dot = _dot_nt if w_is_transposed else functools.partial(jnp.dot, preferred_element_type=F32)
    o_ref[...] = dot(a_ref[...], w_ref[...]).astype(o_ref.dtype)


def _matmul(a, w, out_dtype, bm, bn, name, w_is_transposed=False):
    m, k = a.shape
    n = w.shape[0] if w_is_transposed else w.shape[1]
    bm, bn = _pick_block(bm, m), _pick_block(bn, n)
    blk = _nbytes((bm, k), a.dtype) + _nbytes((k, bn), w.dtype) + _nbytes((bm, bn), out_dtype)
    w_spec = pl.BlockSpec((bn, k), lambda i, j: (j, 0)) if w_is_transposed else pl.BlockSpec((k, bn), lambda i, j: (0, j))
    return pl.pallas_call(
        functools.partial(_mm_kernel, w_is_transposed=w_is_transposed),
        grid=(m // bm, n // bn),
        in_specs=[pl.BlockSpec((bm, k), lambda i, j: (i, 0)), w_spec],
        out_specs=pl.BlockSpec((bm, bn), lambda i, j: (i, j)),
        out_shape=jax.ShapeDtypeStruct((m, n), out_dtype),
        compiler_params=_params(("parallel", "arbitrary"), blk),
        name=name,
    )(a, w)


def _in_proj_kernel(row_tab, a_ref, w_ref, s_ref, o_ref):
    o_ref[...] = (_dot_nt(a_ref[...], w_ref[...]) * s_ref[...]).astype(o_ref.dtype)


def _in_proj(a, wt, segments, col_scale, bm, bn, name):
    m, k = a.shape
    bm = _pick_block(bm, m)
    bn = _pick_block(bn, *(rows for _, rows in segments))
    n = sum(rows for _, rows in segments)
    firsts = [first_row + r for first_row, rows in segments for r in range(0, rows, bn)]
    assert all(f % BF16_SUBLANES == 0 for f in firsts)
    row_tab = jnp.asarray([f // BF16_SUBLANES for f in firsts], jnp.int32)

    grid_spec = pltpu.PrefetchScalarGridSpec(
        num_scalar_prefetch=1,
        grid=(m // bm, n // bn),
        in_specs=[pl.BlockSpec((bm, k), lambda i, j, tab: (i, 0)),
                  pl.BlockSpec((pl.Element(bn), pl.Element(k)), lambda i, j, tab: (tab[j] * BF16_SUBLANES, 0)),
                  pl.BlockSpec((1, bn), lambda i, j, tab: (0, j))],
        out_specs=pl.BlockSpec((bm, bn), lambda i, j, tab: (i, j)),
    )
    blk = _nbytes((bm, k), a.dtype) + _nbytes((bn, k), wt.dtype) + _nbytes((bm, bn), BF16)
    return pl.pallas_call(
        _in_proj_kernel,
        grid_spec=grid_spec,
        out_shape=jax.ShapeDtypeStruct((m, n), BF16),
        compiler_params=_params(("parallel", "arbitrary"), blk),
        name=name,
    )(row_tab, a, wt, col_scale)


def _norm_mm_kernel(x_ref, g_ref, w_ref, o_ref, h_scr):
    @pl.when(pl.program_id(1) == 0)
    def _():
        h_scr[...] = (_rms_scale(x_ref[...]) * g_ref[...]).astype(h_scr.dtype)

    o_ref[...] = jnp.dot(h_scr[...], w_ref[...], preferred_element_type=F32).astype(o_ref.dtype)


def _norm_matmul(x, g, w, out_dtype, bm, bn, name):
    m, k = x.shape
    n = w.shape[1]
    bm, bn = _pick_block(bm, m), _pick_block(bn, n)
    blk = (_nbytes((bm, k), F32) + _nbytes((k, bn), w.dtype) + _nbytes((bm, bn), out_dtype)
           + _nbytes((bm, k), BF16) // 2)
    return pl.pallas_call(
        _norm_mm_kernel,
        grid=(m // bm, n // bn),
        in_specs=[pl.BlockSpec((bm, k), lambda i, j: (i, 0)), pl.BlockSpec((1, k), lambda i, j: (0, 0)),
                  pl.BlockSpec((k, bn), lambda i, j: (0, j))],
        out_specs=pl.BlockSpec((bm, bn), lambda i, j: (i, j)),
        out_shape=jax.ShapeDtypeStruct((m, n), out_dtype),
        scratch_shapes=[pltpu.VMEM((bm, k), BF16)],
        compiler_params=_params(("parallel", "arbitrary"), blk),
        name=name,
    )(x, g.reshape(1, k), w)


def _mm_norm_resid_kernel(a_ref, w_ref, x_ref, g_ref, *rest):
    o_ref = rest[-2] if len(rest) == 3 else rest[0]
    j = pl.program_id(1)
    bn = w_ref.shape[1]
    col = pl.multiple_of(j * bn, bn)
    o_ref[:, pl.ds(col, bn)] = jnp.dot(a_ref[...], w_ref[...], preferred_element_type=F32)

    @pl.when(j == pl.num_programs(1) - 1)
    def _():
        y = x_ref[...] + _rms_scale(o_ref[...]) * g_ref[...]
        o_ref[...] = y
        if len(rest) == 3:
            next_gain_ref, _, h_ref = rest
            h_ref[...] = (_rms_scale(y) * next_gain_ref[...]).astype(h_ref.dtype)


def _matmul_norm_resid(a, w, x, g, bm, bn, name, next_gain=None):
    m, k = a.shape
    n = w.shape[1]
    bm, bn = _pick_block(bm, m), _pick_block(bn, n)
    row = pl.BlockSpec((bm, n), lambda i, j: (i, 0))
    vec = pl.BlockSpec((1, n), lambda i, j: (0, 0))
    in_specs = [pl.BlockSpec((bm, k), lambda i, j: (i, 0)), pl.BlockSpec((k, bn), lambda i, j: (0, j)), row, vec]
    args = [a, w, x, g.reshape(1, n)]
    out_specs, out_shape = row, jax.ShapeDtypeStruct((m, n), F32)
    blk = _nbytes((bm, k), a.dtype) + _nbytes((k, bn), w.dtype) + 2 * _nbytes((bm, n), F32)
    if next_gain is not None:
        in_specs.append(vec)
        args.append(next_gain.reshape(1, n))
        out_specs, out_shape = [row, row], [out_shape, jax.ShapeDtypeStruct((m, n), BF16)]
        blk += _nbytes((bm, n), BF16)
    return pl.pallas_call(
        _mm_norm_resid_kernel,
        grid=(m // bm, n // bn),
        in_specs=in_specs,
        out_specs=out_specs,
        out_shape=out_shape,
        compiler_params=_params(("parallel", "arbitrary"), blk),
        name=name,
    )(*args)


def _fox_bias_kernel(sm_ref, bias_ref, ct_ref):
    blk = LANES
    n_blk = sm_ref.shape[0] // blk
    r_i = lax.broadcasted_iota(jnp.int32, (blk, blk), 0)
    c_i = lax.broadcasted_iota(jnp.int32, (blk, blk), 1)
    tril = (r_i >= c_i).astype(BF16)

    def body(i, carry):
        start = pl.multiple_of(i * blk, blk)
        z = sm_ref[pl.ds(start, blk), :] + bias_ref[...]
        cs = _tril_cumsum(tril, _log_sigmoid(z)) + carry
        ct_ref[pl.ds(start, blk), :] = cs
        return cs[blk - 1:blk, :]

    lax.fori_loop(0, n_blk, body, jnp.zeros((1, blk), F32))


def _fox_bias(small, bias_row, batch, seq):
    tok = pl.BlockSpec((seq, LANES), lambda b: (b, 0))
    return pl.pallas_call(
        _fox_bias_kernel,
        grid=(batch,),
        in_specs=[tok, pl.BlockSpec((1, LANES), lambda b: (0, 0))],
        out_specs=tok,
        out_shape=jax.ShapeDtypeStruct((batch * seq, LANES), F32),
        compiler_params=_params(("parallel",), 2 * _nbytes((seq, LANES), F32)),
        name="fox_bias",
    )(small, bias_row)


def _fox_prep_kernel(q_ref, k_ref, v_ref, cum_ref, qx_ref, kx_ref, vt_ref, *, group, dh):
    rows = q_ref.shape[0]
    lane_c = lax.broadcasted_iota(jnp.int32, cum_ref.shape, 1)
    lane = lax.broadcasted_iota(jnp.int32, (rows, dh), 1)
    cum = cum_ref[...]
    for g in range(group):
        head = pl.program_id(1) * group + g
        c = jnp.sum(jnp.where(lane_c == head, cum, 0.0), axis=1, keepdims=True) * LOG2E
        hi, mid, lo = (t.astype(F32) for t in _split3(c))
        eq = jnp.where(lane == 0, hi, jnp.where(lane == 1, mid, jnp.where(lane == 2, lo,
                       jnp.where(lane < FOX_EXTRA, 1.0, 0.0))))
        ek = jnp.where(lane < 3, 1.0, jnp.where(lane == 3, -hi, jnp.where(lane == 4, -mid,
                       jnp.where(lane == 5, -lo, 0.0))))
        src = slice(g * dh, (g + 1) * dh)
        qx_ref[:, 2 * g * dh:(2 * g + 1) * dh] = q_ref[:, src]
        qx_ref[:, (2 * g + 1) * dh:(2 * g + 2) * dh] = eq.astype(BF16)
        kx_ref[:, 2 * g * dh:(2 * g + 1) * dh] = k_ref[:, src]
        kx_ref[:, (2 * g + 1) * dh:(2 * g + 2) * dh] = ek.astype(BF16)
        vt_ref[g, :dh, :] = v_ref[:, src].astype(F32).T.astype(BF16)
        vt_ref[g, dh:, :] = jnp.ones((vt_ref.shape[1] - dh, rows), BF16)


def _fox_prep(proj, cum_tok, batch, seq, heads, dh, col_q, col_k, col_v, rows=1024):
    t = proj.shape[0]
    group = min(FOX_PREP_GROUP, heads)
    rows = _pick_block(rows, seq)
    per_seq = seq // rows
    gd = group * dh
    cq0, ck0, cv0 = col_q // gd, col_k // gd, col_v // gd
    aug = jax.ShapeDtypeStruct((t, heads * 2 * dh), BF16)
    vrows = dh + BF16_SUBLANES
    blk = (3 * _nbytes((rows, gd), BF16) + _nbytes((rows, LANES), F32) + 2 * _nbytes((rows, 2 * gd), BF16)
           + _nbytes((group, vrows, rows), BF16) + 4 * _nbytes((rows, dh), F32))
    col = lambda c0: pl.BlockSpec((rows, gd), lambda i, h: (i, c0 + h))
    return pl.pallas_call(
        functools.partial(_fox_prep_kernel, group=group, dh=dh),
        grid=(t // rows, heads // group),
        in_specs=[col(cq0), col(ck0), col(cv0), pl.BlockSpec((rows, LANES), lambda i, h: (i, 0))],
        out_specs=[pl.BlockSpec((rows, 2 * gd), lambda i, h: (i, h)),
                   pl.BlockSpec((rows, 2 * gd), lambda i, h: (i, h)),
                   pl.BlockSpec((None, group, vrows, rows), lambda i, h: (i // per_seq, h, 0, i % per_seq))],
        out_shape=[aug, aug, jax.ShapeDtypeStruct((batch, heads, vrows, seq), BF16)],
        compiler_params=_params(("parallel", "arbitrary"), blk),
        name="fox_prep",
    )(proj, proj, proj, cum_tok)


def _fox_kernel(qi_tab, kj_tab, qx_ref, kx_ref, vt_ref, o_ref, m_scr, acc_scr, *, group, dh):
    step = pl.program_id(2)
    qi = qi_tab[step]
    kj = kj_tab[step]
    bq = qx_ref.shape[0]
    bk = kx_ref.shape[0]
    dx = 2 * dh

    @pl.when(kj == 0)
    def _():
        m_scr[...] = jnp.full_like(m_scr, NEG_BIG)
        acc_scr[...] = jnp.zeros_like(acc_scr)

    def update(masked):
        scores = [_dot_nt(kx_ref[:, g * dx:(g + 1) * dx], qx_ref[:, g * dx:(g + 1) * dx])
                  for g in range(group)]
        for g in range(group):
            z = scores[g]
            if masked:
                key = lax.broadcasted_iota(jnp.int32, (bk, bq), 0)
                qry = lax.broadcasted_iota(jnp.int32, (bk, bq), 1)
                z = jnp.where(key <= qry, z, NEG_BIG)
            m_prev = m_scr[g]
            m_new = jnp.maximum(m_prev, jnp.max(z, axis=0, keepdims=True))
            p = jnp.exp2(z - m_new).astype(BF16)
            acc_scr[g] = (jnp.exp2(m_prev - m_new) * acc_scr[g]
                          + jnp.dot(vt_ref[g], p, preferred_element_type=F32))
            m_scr[g] = m_new

    @pl.when(kj != qi)
    def _():
        update(False)

    @pl.when(kj == qi)
    def _():
        update(True)
        for g in range(group):
            acc = acc_scr[g]
            o_ref[:, g * dh:(g + 1) * dh] = (acc[:dh] / acc[dh:dh + 1]).T.astype(o_ref.dtype)


def _fox_attention(qx, kx, vt, batch, seq, heads, dh, bq=512):
    group = min(FOX_GROUP, heads)
    nq = seq // bq
    pairs = [(i, j) for i in range(nq) for j in range(i + 1)]
    qi_tab = jnp.asarray([p[0] for p in pairs], jnp.int32)
    kj_tab = jnp.asarray([p[1] for p in pairs], jnp.int32)
    dxg = group * 2 * dh
    vrows = vt.shape[2]

    grid_spec = pltpu.PrefetchScalarGridSpec(
        num_scalar_prefetch=2,
        grid=(batch, heads // group, len(pairs)),
        in_specs=[
            pl.BlockSpec((bq, dxg), lambda b, h, s, qt, kt: (b * nq + qt[s], h)),
            pl.BlockSpec((bq, dxg), lambda b, h, s, qt, kt: (b * nq + kt[s], h)),
            pl.BlockSpec((None, group, vrows, bq), lambda b, h, s, qt, kt: (b, h, 0, kt[s])),
        ],
        out_specs=pl.BlockSpec((bq, group * dh), lambda b, h, s, qt, kt: (b * nq + qt[s], h)),
        scratch_shapes=[pltpu.VMEM((group, 1, bq), F32), pltpu.VMEM((group, vrows, bq), F32)],
    )
    blk = (2 * _nbytes((bq, dxg), BF16) + _nbytes((group, vrows, bq), BF16) + _nbytes((bq, group * dh), BF16)
           + 6 * _nbytes((bq, bq), F32))
    return pl.pallas_call(
        functools.partial(_fox_kernel, group=group, dh=dh),
        grid_spec=grid_spec,
        out_shape=jax.ShapeDtypeStruct((batch * seq, heads * dh), BF16),
        compiler_params=_params(("parallel", "parallel", "arbitrary"), blk),
        name="fox_attention",
    )(qi_tab, kj_tab, qx, kx, vt)


def _gla_kernel(q_ref, k_ref, v_ref, r_ref, low_ref, wup_ref, bup_ref, gn_ref, o_ref, st_ref, kf_ref, b_ref,
                ball_ref, *, group, dk, dv):
    c, sb = GLA_CHUNK, GLA_SUB
    n_sub = c // sb
    rows_per_step = q_ref.shape[0]
    scale = dk ** -0.5

    @pl.when(pl.program_id(2) == 0)
    def _():
        st_ref[...] = jnp.zeros_like(st_ref)

    r_i = lax.broadcasted_iota(jnp.int32, (c, c), 0)
    c_i = lax.broadcasted_iota(jnp.int32, (c, c), 1)
    tril = (r_i >= c_i).astype(BF16)
    sub_shift = sb.bit_length() - 1
    sub_row = jnp.right_shift(r_i, sub_shift)
    sub_col = jnp.right_shift(c_i, sub_shift)
    lane_sb = lax.broadcasted_iota(jnp.int32, (sb, c), 1)
    row_sb = lax.broadcasted_iota(jnp.int32, (sb, c), 0)

    heads = range(group)
    kcols = [slice(g * dk, (g + 1) * dk) for g in heads]
    vcols = [slice(g * dv, (g + 1) * dv) for g in heads]

    z_all = jnp.dot(low_ref[...].astype(BF16), wup_ref[...], preferred_element_type=F32) + bup_ref[...]
    terms = _split3(_log_sigmoid(z_all) * (LOG2E / GLA_TAU))
    for ci in range(rows_per_step // c):
        rows = slice(ci * c, (ci + 1) * c)
        hi, mid, lo = (jnp.dot(tril, t[rows, :], preferred_element_type=F32) for t in terms)
        ball_ref[rows, :] = hi + mid + lo

    def chunk(ci, carry):
        rows = pl.ds(pl.multiple_of(ci * c, c), c)
        b = [ball_ref[rows, kcols[g]] for g in heads]
        qf = [q_ref[rows, kcols[g]].astype(F32) * scale for g in heads]
        kf = [k_ref[rows, kcols[g]].astype(F32) for g in heads]
        v = [v_ref[rows, vcols[g]] for g in heads]
        st = [st_ref[g] for g in heads]

        o = [_dot_nt((qf[g] * jnp.exp2(b[g])).astype(BF16), st[g].astype(BF16)) for g in heads]

        r = []
        for g in heads:
            kf_ref[g] = kf[g]
            b_ref[g] = b[g]
            ends = [jnp.broadcast_to(b_ref[g, (j + 1) * sb - 1:(j + 1) * sb, :], (c, dk)) for j in range(n_sub)]
            end_of_row = jnp.concatenate([e[:sb] for e in ends], axis=0)
            k_hat = (kf[g] * jnp.exp2(end_of_row - b[g])).astype(BF16)
            q_stack = jnp.concatenate(
                [(qf[g] * jnp.exp2(b[g] - ends[j])).astype(BF16) for j in range(n_sub - 1)],
                axis=0)
            r.append(_dot_nt(q_stack, k_hat))

        att = []
        for g in heads:
            a = jnp.zeros((c, c), F32)
            for j in range(n_sub - 1):
                a = jnp.where((sub_col == j) & (sub_row > j), r[g][j * c:(j + 1) * c, :], a)
            diag = []
            for i in range(n_sub):
                q_i = qf[g][i * sb:(i + 1) * sb, :]
                b_i = b[g][i * sb:(i + 1) * sb, :]
                tile = jnp.zeros((sb, c), F32)
                for j in range(i * sb, (i + 1) * sb):
                    dec = jnp.exp2(b_i - b_ref[g, j:j + 1, :])
                    s = jnp.sum(q_i * kf_ref[g, j:j + 1, :] * dec, axis=-1, keepdims=True)
                    tile = jnp.where(lane_sb == j, s, tile)
                diag.append(jnp.where(lane_sb - i * sb <= row_sb, tile, 0.0))
            att.append((a + jnp.concatenate(diag, axis=0)).astype(BF16))

        for g in heads:
            b_last = b[g][c - 1:c, :]
            o[g] = o[g] + jnp.dot(att[g], v[g], preferred_element_type=F32)
            k_dec = (kf[g] * jnp.exp2(b_last - b[g])).astype(BF16)
            st_ref[g] = st[g] * jnp.exp2(b_last) + _dot_tn(v[g], k_dec)

        for g in heads:
            gate = r_ref[rows, vcols[g]].astype(F32)
            o_ref[rows, vcols[g]] = (_rms_scale(o[g]) * gn_ref[...] * (gate * _sigmoid(gate))).astype(o_ref.dtype)
        return carry

    lax.fori_loop(0, rows_per_step // c, chunk, 0)


def _gla(proj, small, wup_pad, b_up, g_norm, batch, seq, heads, dk, dv, col_q, col_k, col_v, col_r):
    group = min(GLA_GROUP, heads)
    ts = min(GLA_ROWS, seq)
    nt = seq // ts
    gk, gv = group * dk, group * dv
    cq0, ck0, cv0, cr0 = col_q // gk, col_k // gk, col_v // gv, col_r // gv
    blk = 2 * _nbytes((ts, gk), BF16) + 3 * _nbytes((ts, gv), BF16) + _nbytes((ts, LANES), F32)
    return pl.pallas_call(
        functools.partial(_gla_kernel, group=group, dk=dk, dv=dv),
        grid=(batch, heads // group, nt),
        in_specs=[
            pl.BlockSpec((ts, gk), lambda b, h, t: (b * nt + t, cq0 + h)),
            pl.BlockSpec((ts, gk), lambda b, h, t: (b * nt + t, ck0 + h)),
            pl.BlockSpec((ts, gv), lambda b, h, t: (b * nt + t, cv0 + h)),
            pl.BlockSpec((ts, gv), lambda b, h, t: (b * nt + t, cr0 + h)),
            pl.BlockSpec((ts, LANES), lambda b, h, t: (b * nt + t, 0)),
            pl.BlockSpec((LANES, gk), lambda b, h, t: (0, h)),
            pl.BlockSpec((1, gk), lambda b, h, t: (0, h)),
            pl.BlockSpec((1, dv), lambda b, h, t: (0, 0)),
        ],
        out_specs=pl.BlockSpec((ts, gv), lambda b, h, t: (b * nt + t, h)),
        out_shape=jax.ShapeDtypeStruct((batch * seq, heads * dv), BF16),
        scratch_shapes=[pltpu.VMEM((group, dv, dk), F32), pltpu.VMEM((group, GLA_CHUNK, dk), F32),
                        pltpu.VMEM((group, GLA_CHUNK, dk), F32), pltpu.VMEM((ts, gk), F32)],
        compiler_params=_params(("parallel", "parallel", "arbitrary"), blk),
        name="gla",
    )(proj, proj, proj, proj, small, wup_pad, b_up, g_norm)


def _merge_kernel(a_ref, wa_ref, f_ref, wf_ref, g0_ref, g1_ref, b0_ref, b1_ref, o_ref):
    ya = jnp.dot(a_ref[...], wa_ref[...], preferred_element_type=F32)
    yb = jnp.dot(f_ref[...], wf_ref[...], preferred_element_type=F32)
    g0 = _sigmoid(g0_ref[...].astype(F32) + b0_ref[...])
    g1 = _sigmoid(g1_ref[...].astype(F32) + b1_ref[...])
    o_ref[...] = (g0 * ya + g1 * yb).astype(o_ref.dtype)


def _merge(o_gla, w_a, o_fox, w_f, proj, b_gate, col_gates, bm=1024, bn=512):
    m, ka = o_gla.shape
    kf = o_fox.shape[1]
    d = w_a.shape[1]
    bm, bn = _pick_block(bm, m), _pick_block(bn, d, col_gates)
    g0, nb = col_gates // bn, d // bn
    blk = (_nbytes((bm, ka), BF16) + _nbytes((bm, kf), BF16) + _nbytes((ka, bn), BF16)
           + _nbytes((kf, bn), BF16) + 3 * _nbytes((bm, bn), BF16))
    return pl.pallas_call(
        _merge_kernel,
        grid=(m // bm, nb),
        in_specs=[
            pl.BlockSpec((bm, ka), lambda i, j: (i, 0)),
            pl.BlockSpec((ka, bn), lambda i, j: (0, j)),
            pl.BlockSpec((bm, kf), lambda i, j: (i, 0)),
            pl.BlockSpec((kf, bn), lambda i, j: (0, j)),
            pl.BlockSpec((bm, bn), lambda i, j: (i, g0 + j)),
            pl.BlockSpec((bm, bn), lambda i, j: (i, g0 + nb + j)),
            pl.BlockSpec((1, bn), lambda i, j: (0, j)),
            pl.BlockSpec((1, bn), lambda i, j: (0, nb + j)),
        ],
        out_specs=pl.BlockSpec((bm, bn), lambda i, j: (i, j)),
        out_shape=jax.ShapeDtypeStruct((m, d), BF16),
        compiler_params=_params(("parallel", "arbitrary"), blk),
        name="branch_merge",
    )(o_gla, w_a, o_fox, w_f, proj, proj, b_gate, b_gate)


def _xattn_kernel(q_ref, kv_ref, o_ref, *, heads, dh):
    for h in range(heads):
        q = q_ref[:, h * dh:(h + 1) * dh]
        k = kv_ref[:, h * dh:(h + 1) * dh]
        v = kv_ref[:, (heads + h) * dh:(heads + h + 1) * dh]
        s = _dot_nt(q, k) * (dh ** -0.5)
        p = jnp.exp(s - jnp.max(s, axis=-1, keepdims=True))
        l = jnp.sum(p, axis=-1, keepdims=True)
        o = jnp.dot(p.astype(BF16), v, preferred_element_type=F32) / l
        o_ref[:, h * dh:(h + 1) * dh] = o.astype(o_ref.dtype)


def _xattn(q, kv, batch, seq, mem_len, heads, dh, bq=512):
    nq = seq // bq
    w = heads * dh
    blk = 2 * _nbytes((bq, w), BF16) + _nbytes((mem_len, 2 * w), BF16) + 4 * _nbytes((bq, mem_len), F32)
    return pl.pallas_call(
        functools.partial(_xattn_kernel, heads=heads, dh=dh),
        grid=(batch, nq),
        in_specs=[pl.BlockSpec((bq, w), lambda b, i: (b * nq + i, 0)),
                  pl.BlockSpec((mem_len, 2 * w), lambda b, i: (b, 0))],
        out_specs=pl.BlockSpec((bq, w), lambda b, i: (b * nq + i, 0)),
        out_shape=jax.ShapeDtypeStruct((batch * seq, w), BF16),
        compiler_params=_params(("parallel", "arbitrary"), blk),
        name="cross_attention",
    )(q, kv)


def _ffn_up_kernel(a_ref, wg_ref, wu_ref, cg_ref, cu_ref, bg_ref, bu_ref, o_ref, w_scr, raw_ref,
                   *, blocks_per_seq):
    i = pl.program_id(1)
    bm = a_ref.shape[0]

    @pl.when(i == 0)
    def _():
        w_scr[0] = wg_ref[...].astype(BF16)
        w_scr[1] = wu_ref[...].astype(BF16)

    @pl.when(lax.rem(i, blocks_per_seq) == 0)
    def _():
        raw_ref[:, :SUBLANES, :] = jnp.zeros((2, SUBLANES, raw_ref.shape[2]), F32)

    a = a_ref[...]

    def conv(slot, c_ref, b_ref):
        raw_ref[slot, SUBLANES:, :] = jnp.dot(a, w_scr[slot], preferred_element_type=F32)
        x0 = raw_ref[slot, SUBLANES:, :]
        x1 = raw_ref[slot, SUBLANES - 1:SUBLANES - 1 + bm, :]
        x2 = raw_ref[slot, SUBLANES - 2:SUBLANES - 2 + bm, :]
        raw_ref[slot, :SUBLANES, :] = raw_ref[slot, bm:, :]
        return c_ref[0:1, :] * x2 + c_ref[1:2, :] * x1 + c_ref[2:3, :] * x0 + b_ref[...]

    g = conv(0, cg_ref, bg_ref)
    u = conv(1, cu_ref, bu_ref)
    gelu = 0.5 * g * (1.0 + jnp.tanh(GELU_C * (g + 0.044715 * (g * g * g))))
    o_ref[...] = (gelu * u).astype(o_ref.dtype)


def _ffn_up(h, w_up, w_conv, b_conv, seq, d_ff, bm=1024, bn=256):
    m, k = h.shape
    bm, bn = _pick_block(bm, seq), _pick_block(bn, d_ff)
    nb = d_ff // bn
    taps = w_conv.shape[0]
    wsp = lambda off: pl.BlockSpec((k, bn), lambda j, i: (0, off + j))
    csp = lambda off: pl.BlockSpec((taps, bn), lambda j, i: (0, off + j))
    bsp = lambda off: pl.BlockSpec((1, bn), lambda j, i: (0, off + j))
    scratch = 2 * _nbytes((k, bn), BF16) + 8 * _nbytes((bm, bn), F32)
    blk = _nbytes((bm, k), BF16) + 2 * _nbytes((k, bn), F32) + _nbytes((bm, bn), BF16) + scratch // 2
    return pl.pallas_call(
        functools.partial(_ffn_up_kernel, blocks_per_seq=seq // bm),
        grid=(nb, m // bm),
        in_specs=[pl.BlockSpec((bm, k), lambda j, i: (i, 0)), wsp(0), wsp(nb), csp(0), csp(nb), bsp(0), bsp(nb)],
        out_specs=pl.BlockSpec((bm, bn), lambda j, i: (i, j)),
        out_shape=jax.ShapeDtypeStruct((m, d_ff), BF16),
        scratch_shapes=[pltpu.VMEM((2, k, bn), BF16), pltpu.VMEM((2, SUBLANES + bm, bn), F32)],
        compiler_params=_params(("parallel", "arbitrary"), blk),
        name="ffn_up",
    )(h, w_up, w_up, w_conv, w_conv, b_conv, b_conv)


def _layer(x, mem, g_mix_pre, w_in, b_gate, w_gla_gate_up, b_gla_gate, g_gla_norm, b_fox_f,
           w_gla_branch, w_fox_branch, w_out, g_mix_post, g_xa_pre, g_mem, w_xa_q, w_xa_kv,
           w_xa_o, g_xa_post, g_ffn_pre, w_ffn_up, w_conv, b_conv, w_ffn_down, g_ffn_post,
           batch, seq):
    d_model = x.shape[1]
    gla_qk = w_gla_gate_up.shape[1]
    gla_dv = g_gla_norm.shape[0]
    gla_vw = w_gla_branch.shape[0]
    gla_heads = gla_vw // gla_dv
    gla_dk = gla_qk // gla_heads
    fox_w = w_fox_branch.shape[0]
    fox_heads = b_fox_f.shape[0]
    fox_dh = fox_w // fox_heads
    xa_w = w_xa_q.shape[1]
    mem_len = mem.shape[0] // batch
    xa_heads = XA_HEADS
    xa_dh = xa_w // xa_heads
    d_ff = w_ffn_down.shape[0]
    assert fox_heads + GLA_RANK <= LANES and gla_dk == LANES and fox_dh == LANES

    sizes = (gla_qk, gla_qk, gla_vw, gla_vw, GLA_RANK, fox_w, fox_w, fox_w, fox_heads, 2 * d_model)
    offs = [0]
    for s in sizes:
        offs.append(offs[-1] + s)
    o_gq, o_gk, o_gv, o_gr, o_low, o_fq, o_fk, o_fv, o_ff, o_gates = offs[:-1]
    wt = w_in.T.astype(BF16)
    wide_segments = [(o_gq, o_low - o_gq), (o_fq, o_ff - o_fq), (o_gates, 2 * d_model)]
    w_small = jnp.concatenate([wt[o_ff:o_ff + fox_heads], wt[o_low:o_low + GLA_RANK],
                               jnp.zeros((LANES - fox_heads - GLA_RANK, d_model), BF16)], axis=0)
    c_gq, c_gk, c_gv, c_gr = o_gq, o_gk, o_gv, o_gr
    c_fq = o_low
    c_fk, c_fv = c_fq + fox_w, c_fq + 2 * fox_w
    c_gates = c_fq + 3 * fox_w
    col_scale = jnp.ones((1, c_gates + 2 * d_model), F32).at[:, c_fq:c_fk].set(LOG2E * fox_dh ** -0.5)
    wup_pad = jnp.zeros((LANES, gla_qk), F32).at[fox_heads:fox_heads + GLA_RANK].set(w_gla_gate_up).astype(BF16)
    fox_bias_row = jnp.zeros((1, LANES), F32).at[0, :fox_heads].set(b_fox_f)

    h = _rmsnorm(x, g_mix_pre)
    proj = _in_proj(h, wt, wide_segments, col_scale, 1024, 1024, "in_proj")
    small = _matmul(h, w_small, F32, 1024, LANES, "in_proj_small", w_is_transposed=True)
    o_gla = _gla(proj, small, wup_pad, b_gla_gate.reshape(1, -1), g_gla_norm.reshape(1, -1),
                 batch, seq, gla_heads, gla_dk, gla_dv, c_gq, c_gk, c_gv, c_gr)
    cum_tok = _fox_bias(small, fox_bias_row, batch, seq)
    qx, kx, vt = _fox_prep(proj, cum_tok, batch, seq, fox_heads, fox_dh, c_fq, c_fk, c_fv)
    o_fox = _fox_attention(qx, kx, vt, batch, seq, fox_heads, fox_dh)
    merged = _merge(o_gla, w_gla_branch.astype(BF16), o_fox, w_fox_branch.astype(BF16), proj,
                    b_gate.reshape(1, -1), c_gates)
    x = _matmul_norm_resid(merged, w_out.astype(BF16), x, g_mix_post, 512, 512, "mix_out")

    q = _norm_matmul(x, g_xa_pre, w_xa_q.astype(BF16), BF16, 512, 1024, "xa_q")
    kv = _norm_matmul(mem, g_mem, w_xa_kv.astype(BF16), BF16, 512, 1024, "xa_kv")
    o = _xattn(q, kv, batch, seq, mem_len, xa_heads, xa_dh)
    x, h = _matmul_norm_resid(o, w_xa_o.astype(BF16), x, g_xa_post, 256, d_model, "xa_out",
                              next_gain=g_ffn_pre)

    act = _ffn_up(h, w_ffn_up, w_conv, b_conv.reshape(1, -1), seq, d_ff)
    y = _matmul(act, w_ffn_down.astype(BF16), F32, 512, 512, "ffn_down")
    return _norm_resid(y, x, g_ffn_post)


def kernel(x, mem, g_mix_pre, w_in, b_gate, w_gla_gate_up, b_gla_gate, g_gla_norm, b_fox_f, w_gla_branch, w_fox_branch, w_out, g_mix_post, g_xa_pre, g_mem, w_xa_q, w_xa_kv, w_xa_o, g_xa_post, g_ffn_pre, w_ffn_up, w_conv, b_conv, w_ffn_down, g_ffn_post):
    batch, seq, d_model = x.shape
    xf = x.reshape(batch * seq, d_model)
    mf = mem.reshape(-1, d_model)
    per_layer = (g_mix_pre, w_in, b_gate, w_gla_gate_up, b_gla_gate, g_gla_norm, b_fox_f, w_gla_branch,
                 w_fox_branch, w_out, g_mix_post, g_xa_pre, g_mem, w_xa_q, w_xa_kv, w_xa_o, g_xa_post,
                 g_ffn_pre, w_ffn_up, w_conv, b_conv, w_ffn_down, g_ffn_post)
    for l in range(w_in.shape[0]):
        xf = _layer(xf, mf, *(p[l] for p in per_layer), batch=batch, seq=seq)
    return xf.reshape(batch, seq, d_model)
```

```python
import functools
import math

import jax
import jax.numpy as jnp
from jax import lax
from jax.experimental import pallas as pl
from jax.experimental.pallas import tpu as pltpu

F32 = jnp.float32
BF16 = jnp.bfloat16

LANES = 128
SUBLANES = 8
BF16_SUBLANES = 16
V7X_VMEM_BYTES = 64 * 1024 * 1024
VMEM_CAP_BYTES = V7X_VMEM_BYTES - 8 * 1024 * 1024

EPS = 1e-6
GLA_TAU = 16.0
GLA_CHUNK = 64
GLA_SUB = 8
GLA_RANK = 16
GLA_GROUP = 8
GLA_ROWS = 1024
FOX_GROUP = 16
FOX_PREP_GROUP = 4
FOX_EXTRA = 6
XA_HEADS = 4
GELU_C = math.sqrt(2.0 / math.pi)
LOG2E = math.log2(math.e)
NEG_BIG = -1e30


def _params(semantics, block_bytes):
    limit = min(VMEM_CAP_BYTES, 2 * block_bytes + 16 * 1024 * 1024)
    return pltpu.CompilerParams(dimension_semantics=semantics, vmem_limit_bytes=int(limit))


def _nbytes(shape, dtype):
    return math.prod(shape) * jnp.dtype(dtype).itemsize


def _pick_block(target, *dims):
    g = math.gcd(*dims)
    best = None
    for b in range(LANES, min(target, g) + 1, LANES):
        if g % b == 0:
            best = b
    assert best is not None, (target, dims)
    return best


def _log_sigmoid(z):
    return jnp.minimum(z, 0.0) - jnp.log(1.0 + jnp.exp(-jnp.abs(z)))


def _sigmoid(z):
    return 1.0 / (1.0 + jnp.exp(-z))


def _split3(x):
    hi = x.astype(BF16)
    r1 = x - hi.astype(F32)
    mid = r1.astype(BF16)
    lo = (r1 - mid.astype(F32)).astype(BF16)
    return hi, mid, lo


def _tril_cumsum(tril, x):
    hi, mid, lo = _split3(x)
    dot = functools.partial(jnp.dot, preferred_element_type=F32)
    return dot(tril, hi) + dot(tril, mid) + dot(tril, lo)


def _dot_nt(a, b):
    return lax.dot_general(a, b, (((1,), (1,)), ((), ())), preferred_element_type=F32)


def _dot_tn(a, b):
    return lax.dot_general(a, b, (((0,), (0,)), ((), ())), preferred_element_type=F32)


def _rms_scale(y):
    return y * lax.rsqrt(jnp.mean(y * y, axis=-1, keepdims=True) + EPS)


def _rmsnorm_side_kernel(x_ref, g_ref, wt_ref, o_ref, side_ref):
    h = (_rms_scale(x_ref[...]) * g_ref[...]).astype(o_ref.dtype)
    o_ref[...] = h
    side_ref[...] = _dot_nt(h, wt_ref[...])


def _rmsnorm_side(x, g, wt, bm=256):
    m, d = x.shape
    n = wt.shape[0]
    blk = _nbytes((bm, d), F32) + _nbytes((bm, d), BF16) + _nbytes((n, d), BF16) + _nbytes((bm, n), F32)
    return pl.pallas_call(
        _rmsnorm_side_kernel,
        grid=(m // bm,),
        in_specs=[pl.BlockSpec((bm, d), lambda i: (i, 0)), pl.BlockSpec((1, d), lambda i: (0, 0)),
                  pl.BlockSpec((n, d), lambda i: (0, 0))],
        out_specs=[pl.BlockSpec((bm, d), lambda i: (i, 0)), pl.BlockSpec((bm, n), lambda i: (i, 0))],
        out_shape=[jax.ShapeDtypeStruct((m, d), BF16), jax.ShapeDtypeStruct((m, n), F32)],
        compiler_params=_params(("parallel",), blk),
        name="rmsnorm",
    )(x, g.reshape(1, d), wt)


def _norm_resid_kernel(y_ref, x_ref, g_ref, o_ref):
    o_ref[...] = x_ref[...] + _rms_scale(y_ref[...]) * g_ref[...]


def _norm_resid(y, x, g, bm=256):
    m, d = x.shape
    blk = 3 * _nbytes((bm, d), F32)
    row = pl.BlockSpec((bm, d), lambda i: (i, 0))
    return pl.pallas_call(
        _norm_resid_kernel,
        grid=(m // bm,),
        in_specs=[row, row, pl.BlockSpec((1, d), lambda i: (0, 0))],
        out_specs=row,
        out_shape=jax.ShapeDtypeStruct((m, d), F32),
        compiler_params=_params(("parallel",), blk),
        name="norm_resid",
    )(y, x, g.reshape(1, d))


def _mm_kernel(a_ref, w_ref, o_ref):
    o_ref[...] = jnp.dot(a_ref[...], w_ref[...], preferred_element_type=F32).astype(o_ref.dtype)


def _matmul(a, w, out_dtype, bm, bn, name):
    m, k = a.shape
    n = w.shape[1]
    bm, bn = _pick_block(bm, m), _pick_block(bn, n)
    blk = _nbytes((bm, k), a.dtype) + _nbytes((k, bn), w.dtype) + _nbytes((bm, bn), out_dtype)
    return pl.pallas_call(
        _mm_kernel,
        grid=(m // bm, n // bn),
        in_specs=[pl.BlockSpec((bm, k), lambda i, j: (i, 0)), pl.BlockSpec((k, bn), lambda i, j: (0, j))],
        out_specs=pl.BlockSpec((bm, bn), lambda i, j: (i, j)),
        out_shape=jax.ShapeDtypeStruct((m, n), out_dtype),
        compiler_params=_params(("parallel", "arbitrary"), blk),
        name=name,
    )(a, w)


def _in_proj_kernel(row_tab, a_ref, w_ref, s_ref, o_ref):
    o_ref[...] = (_dot_nt(a_ref[...], w_ref[...]) * s_ref[...]).astype(o_ref.dtype)


def _in_proj(a, wt, segments, col_scale, bm, bn, name):
    m, k = a.shape
    bm = _pick_block(bm, m)
    bn = _pick_block(bn, *(rows for _, rows in segments))
    n = sum(rows for _, rows in segments)
    firsts = [first_row + r for first_row, rows in segments for r in range(0, rows, bn)]
    assert all(f % BF16_SUBLANES == 0 for f in firsts)
    row_tab = jnp.asarray([f // BF16_SUBLANES for f in firsts], jnp.int32)

    grid_spec = pltpu.PrefetchScalarGridSpec(
        num_scalar_prefetch=1,
        grid=(m // bm, n // bn),
        in_specs=[pl.BlockSpec((bm, k), lambda i, j, tab: (i, 0)),
                  pl.BlockSpec((pl.Element(bn), pl.Element(k)), lambda i, j, tab: (tab[j] * BF16_SUBLANES, 0)),
                  pl.BlockSpec((1, bn), lambda i, j, tab: (0, j))],
        out_specs=pl.BlockSpec((bm, bn), lambda i, j, tab: (i, j)),
    )
    blk = _nbytes((bm, k), a.dtype) + _nbytes((bn, k), wt.dtype) + _nbytes((bm, bn), BF16)
    return pl.pallas_call(
        _in_proj_kernel,
        grid_spec=grid_spec,
        out_shape=jax.ShapeDtypeStruct((m, n), BF16),
        compiler_params=_params(("parallel", "arbitrary"), blk),
        name=name,
    )(row_tab, a, wt, col_scale)


def _norm_mm_kernel(x_ref, g_ref, w_ref, o_ref, h_scr):
    @pl.when(pl.program_id(1) == 0)
    def _():
        h_scr[...] = (_rms_scale(x_ref[...]) * g_ref[...]).astype(h_scr.dtype)

    o_ref[...] = jnp.dot(h_scr[...], w_ref[...], preferred_element_type=F32).astype(o_ref.dtype)


def _norm_matmul(x, g, w, out_dtype, bm, bn, name):
    m, k = x.shape
    n = w.shape[1]
    bm, bn = _pick_block(bm, m), _pick_block(bn, n)
    blk = (_nbytes((bm, k), F32) + _nbytes((k, bn), w.dtype) + _nbytes((bm, bn), out_dtype)
           + _nbytes((bm, k), BF16) // 2)
    return pl.pallas_call(
        _norm_mm_kernel,
        grid=(m // bm, n // bn),
        in_specs=[pl.BlockSpec((bm, k), lambda i, j: (i, 0)), pl.BlockSpec((1, k), lambda i, j: (0, 0)),
                  pl.BlockSpec((k, bn), lambda i, j: (0, j))],
        out_specs=pl.BlockSpec((bm, bn), lambda i, j: (i, j)),
        out_shape=jax.ShapeDtypeStruct((m, n), out_dtype),
        scratch_shapes=[pltpu.VMEM((bm, k), BF16)],
        compiler_params=_params(("parallel", "arbitrary"), blk),
        name=name,
    )(x, g.reshape(1, k), w)


def _mm_norm_resid_kernel(a_ref, w_ref, x_ref, g_ref, *rest):
    o_ref = rest[-2] if len(rest) == 3 else rest[0]
    j = pl.program_id(1)
    bn = w_ref.shape[1]
    col = pl.multiple_of(j * bn, bn)
    o_ref[:, pl.ds(col, bn)] = jnp.dot(a_ref[...], w_ref[...], preferred_element_type=F32)

    @pl.when(j == pl.num_programs(1) - 1)
    def _():
        y = x_ref[...] + _rms_scale(o_ref[...]) * g_ref[...]
        o_ref[...] = y
        if len(rest) == 3:
            next_gain_ref, _, h_ref = rest
            h_ref[...] = (_rms_scale(y) * next_gain_ref[...]).astype(h_ref.dtype)


def _matmul_norm_resid(a, w, x, g, bm, bn, name, next_gain=None):
    m, k = a.shape
    n = w.shape[1]
    bm, bn = _pick_block(bm, m), _pick_block(bn, n)
    row = pl.BlockSpec((bm, n), lambda i, j: (i, 0))
    vec = pl.BlockSpec((1, n), lambda i, j: (0, 0))
    in_specs = [pl.BlockSpec((bm, k), lambda i, j: (i, 0)), pl.BlockSpec((k, bn), lambda i, j: (0, j)), row, vec]
    args = [a, w, x, g.reshape(1, n)]
    out_specs, out_shape = row, jax.ShapeDtypeStruct((m, n), F32)
    blk = _nbytes((bm, k), a.dtype) + _nbytes((k, bn), w.dtype) + 2 * _nbytes((bm, n), F32)
    if next_gain is not None:
        in_specs.append(vec)
        args.append(next_gain.reshape(1, n))
        out_specs, out_shape = [row, row], [out_shape, jax.ShapeDtypeStruct((m, n), BF16)]
        blk += _nbytes((bm, n), BF16)
    return pl.pallas_call(
        _mm_norm_resid_kernel,
        grid=(m // bm, n // bn),
        in_specs=in_specs,
        out_specs=out_specs,
        out_shape=out_shape,
        compiler_params=_params(("parallel", "arbitrary"), blk),
        name=name,
    )(*args)


def _fox_bias_kernel(sm_ref, bias_ref, ct_ref):
    blk = LANES
    n_blk = sm_ref.shape[0] // blk
    r_i = lax.broadcasted_iota(jnp.int32, (blk, blk), 0)
    c_i = lax.broadcasted_iota(jnp.int32, (blk, blk), 1)
    tril = (r_i >= c_i).astype(BF16)

    def body(i, carry):
        start = pl.multiple_of(i * blk, blk)
        z = sm_ref[pl.ds(start, blk), :] + bias_ref[...]
        cs = _tril_cumsum(tril, _log_sigmoid(z)) + carry
        ct_ref[pl.ds(start, blk), :] = cs
        return cs[blk - 1:blk, :]

    lax.fori_loop(0, n_blk, body, jnp.zeros((1, blk), F32))


def _fox_bias(small, bias_row, batch, seq):
    tok = pl.BlockSpec((seq, LANES), lambda b: (b, 0))
    return pl.pallas_call(
        _fox_bias_kernel,
        grid=(batch,),
        in_specs=[tok, pl.BlockSpec((1, LANES), lambda b: (0, 0))],
        out_specs=tok,
        out_shape=jax.ShapeDtypeStruct((batch * seq, LANES), F32),
        compiler_params=_params(("parallel",), 2 * _nbytes((seq, LANES), F32)),
        name="fox_bias",
    )(small, bias_row)


def _fox_prep_kernel(q_ref, k_ref, v_ref, cum_ref, qx_ref, kx_ref, vt_ref, *, group, dh):
    rows = q_ref.shape[0]
    lane_c = lax.broadcasted_iota(jnp.int32, cum_ref.shape, 1)
    lane = lax.broadcasted_iota(jnp.int32, (rows, dh), 1)
    cum = cum_ref[...]
    for g in range(group):
        head = pl.program_id(1) * group + g
        c = jnp.sum(jnp.where(lane_c == head, cum, 0.0), axis=1, keepdims=True) * LOG2E
        hi, mid, lo = (t.astype(F32) for t in _split3(c))
        eq = jnp.where(lane == 0, hi, jnp.where(lane == 1, mid, jnp.where(lane == 2, lo,
                       jnp.where(lane < FOX_EXTRA, 1.0, 0.0))))
        ek = jnp.where(lane < 3, 1.0, jnp.where(lane == 3, -hi, jnp.where(lane == 4, -mid,
                       jnp.where(lane == 5, -lo, 0.0))))
        src = slice(g * dh, (g + 1) * dh)
        qx_ref[:, 2 * g * dh:(2 * g + 1) * dh] = q_ref[:, src]
        qx_ref[:, (2 * g + 1) * dh:(2 * g + 2) * dh] = eq.astype(BF16)
        kx_ref[:, 2 * g * dh:(2 * g + 1) * dh] = k_ref[:, src]
        kx_ref[:, (2 * g + 1) * dh:(2 * g + 2) * dh] = ek.astype(BF16)
        vt_ref[g, :dh, :] = v_ref[:, src].astype(F32).T.astype(BF16)
        vt_ref[g, dh:, :] = jnp.ones((vt_ref.shape[1] - dh, rows), BF16)


def _fox_prep(proj, cum_tok, batch, seq, heads, dh, col_q, col_k, col_v, rows=1024):
    t = proj.shape[0]
    group = min(FOX_PREP_GROUP, heads)
    rows = _pick_block(rows, seq)
    per_seq = seq // rows
    gd = group * dh
    cq0, ck0, cv0 = col_q // gd, col_k // gd, col_v // gd
    aug = jax.ShapeDtypeStruct((t, heads * 2 * dh), BF16)
    vrows = dh + BF16_SUBLANES
    blk = (3 * _nbytes((rows, gd), BF16) + _nbytes((rows, LANES), F32) + 2 * _nbytes((rows, 2 * gd), BF16)
           + _nbytes((group, vrows, rows), BF16) + 4 * _nbytes((rows, dh), F32))
    col = lambda c0: pl.BlockSpec((rows, gd), lambda i, h: (i, c0 + h))
    return pl.pallas_call(
        functools.partial(_fox_prep_kernel, group=group, dh=dh),
        grid=(t // rows, heads // group),
        in_specs=[col(cq0), col(ck0), col(cv0), pl.BlockSpec((rows, LANES), lambda i, h: (i, 0))],
        out_specs=[pl.BlockSpec((rows, 2 * gd), lambda i, h: (i, h)),
                   pl.BlockSpec((rows, 2 * gd), lambda i, h: (i, h)),
                   pl.BlockSpec((None, group, vrows, rows), lambda i, h: (i // per_seq, h, 0, i % per_seq))],
        out_shape=[aug, aug, jax.ShapeDtypeStruct((batch, heads, vrows, seq), BF16)],
        compiler_params=_params(("parallel", "arbitrary"), blk),
        name="fox_prep",
    )(proj, proj, proj, cum_tok)


def _fox_kernel(qi_tab, kj_tab, qx_ref, kx_ref, vt_ref, o_ref, m_scr, acc_scr, *, group, dh):
    step = pl.program_id(2)
    qi = qi_tab[step]
    kj = kj_tab[step]
    bq = qx_ref.shape[0]
    bk = kx_ref.shape[0]
    dx = 2 * dh

    @pl.when(kj == 0)
    def _():
        m_scr[...] = jnp.full_like(m_scr, NEG_BIG)
        acc_scr[...] = jnp.zeros_like(acc_scr)

    def update(masked):
        scores = [_dot_nt(kx_ref[:, g * dx:(g + 1) * dx], qx_ref[:, g * dx:(g + 1) * dx])
                  for g in range(group)]
        for g in range(group):
            z = scores[g]
            if masked:
                key = lax.broadcasted_iota(jnp.int32, (bk, bq), 0)
                qry = lax.broadcasted_iota(jnp.int32, (bk, bq), 1)
                z = jnp.where(key <= qry, z, NEG_BIG)
            m_prev = m_scr[g]
            m_new = jnp.maximum(m_prev, jnp.max(z, axis=0, keepdims=True))
            p = jnp.exp2(z - m_new).astype(BF16)
            acc_scr[g] = (jnp.exp2(m_prev - m_new) * acc_scr[g]
                          + jnp.dot(vt_ref[g], p, preferred_element_type=F32))
            m_scr[g] = m_new

    @pl.when(kj != qi)
    def _():
        update(False)

    @pl.when(kj == qi)
    def _():
        update(True)
        for g in range(group):
            acc = acc_scr[g]
            o_ref[:, g * dh:(g + 1) * dh] = (acc[:dh] / acc[dh:dh + 1]).T.astype(o_ref.dtype)


def _fox_attention(qx, kx, vt, batch, seq, heads, dh, bq=512):
    group = min(FOX_GROUP, heads)
    nq = seq // bq
    pairs = [(i, j) for i in range(nq) for j in range(i + 1)]
    qi_tab = jnp.asarray([p[0] for p in pairs], jnp.int32)
    kj_tab = jnp.asarray([p[1] for p in pairs], jnp.int32)
    dxg = group * 2 * dh
    vrows = vt.shape[2]

    grid_spec = pltpu.PrefetchScalarGridSpec(
        num_scalar_prefetch=2,
        grid=(batch, heads // group, len(pairs)),
        in_specs=[
            pl.BlockSpec((bq, dxg), lambda b, h, s, qt, kt: (b * nq + qt[s], h)),
            pl.BlockSpec((bq, dxg), lambda b, h, s, qt, kt: (b * nq + kt[s], h)),
            pl.BlockSpec((None, group, vrows, bq), lambda b, h, s, qt, kt: (b, h, 0, kt[s])),
        ],
        out_specs=pl.BlockSpec((bq, group * dh), lambda b, h, s, qt, kt: (b * nq + qt[s], h)),
        scratch_shapes=[pltpu.VMEM((group, 1, bq), F32), pltpu.VMEM((group, vrows, bq), F32)],
    )
    blk = (2 * _nbytes((bq, dxg), BF16) + _nbytes((group, vrows, bq), BF16) + _nbytes((bq, group * dh), BF16)
           + 6 * _nbytes((bq, bq), F32))
    return pl.pallas_call(
        functools.partial(_fox_kernel, group=group, dh=dh),
        grid_spec=grid_spec,
        out_shape=jax.ShapeDtypeStruct((batch * seq, heads * dh), BF16),
        compiler_params=_params(("parallel", "parallel", "arbitrary"), blk),
        name="fox_attention",
    )(qi_tab, kj_tab, qx, kx, vt)


def _gla_kernel(q_ref, k_ref, v_ref, r_ref, low_ref, wup_ref, bup_ref, gn_ref, o_ref, st_ref, kf_ref, b_ref,
                ball_ref, *, group, dk, dv):
    c, sb = GLA_CHUNK, GLA_SUB
    n_sub = c // sb
    rows_per_step = q_ref.shape[0]
    scale = dk ** -0.5

    @pl.when(pl.program_id(2) == 0)
    def _():
        st_ref[...] = jnp.zeros_like(st_ref)

    r_i = lax.broadcasted_iota(jnp.int32, (c, c), 0)
    c_i = lax.broadcasted_iota(jnp.int32, (c, c), 1)
    tril = (r_i >= c_i).astype(BF16)
    sub_shift = sb.bit_length() - 1
    sub_row = jnp.right_shift(r_i, sub_shift)
    sub_col = jnp.right_shift(c_i, sub_shift)
    lane_sb = lax.broadcasted_iota(jnp.int32, (sb, c), 1)
    row_sb = lax.broadcasted_iota(jnp.int32, (sb, c), 0)

    heads = range(group)
    kcols = [slice(g * dk, (g + 1) * dk) for g in heads]
    vcols = [slice(g * dv, (g + 1) * dv) for g in heads]

    z_all = jnp.dot(low_ref[...].astype(BF16), wup_ref[...], preferred_element_type=F32) + bup_ref[...]
    terms = _split3(_log_sigmoid(z_all) * (LOG2E / GLA_TAU))
    for ci in range(rows_per_step // c):
        rows = slice(ci * c, (ci + 1) * c)
        hi, mid, lo = (jnp.dot(tril, t[rows, :], preferred_element_type=F32) for t in terms)
        ball_ref[rows, :] = hi + mid + lo

    def chunk(ci, carry):
        rows = pl.ds(pl.multiple_of(ci * c, c), c)
        b = [ball_ref[rows, kcols[g]] for g in heads]
        qf = [q_ref[rows, kcols[g]].astype(F32) * scale for g in heads]
        kf = [k_ref[rows, kcols[g]].astype(F32) for g in heads]
        v = [v_ref[rows, vcols[g]] for g in heads]
        st = [st_ref[g] for g in heads]

        o = [_dot_nt((qf[g] * jnp.exp2(b[g])).astype(BF16), st[g].astype(BF16)) for g in heads]

        r = []
        for g in heads:
            kf_ref[g] = kf[g]
            b_ref[g] = b[g]
            ends = [jnp.broadcast_to(b_ref[g, (j + 1) * sb - 1:(j + 1) * sb, :], (c, dk)) for j in range(n_sub)]
            end_of_row = jnp.concatenate([e[:sb] for e in ends], axis=0)
            k_hat = (kf[g] * jnp.exp2(end_of_row - b[g])).astype(BF16)
            q_stack = jnp.concatenate(
                [(qf[g] * jnp.exp2(b[g] - ends[j])).astype(BF16) for j in range(n_sub - 1)],
                axis=0)
            r.append(_dot_nt(q_stack, k_hat))

        att = []
        for g in heads:
            a = jnp.zeros((c, c), F32)
            for j in range(n_sub - 1):
                a = jnp.where((sub_col == j) & (sub_row > j), r[g][j * c:(j + 1) * c, :], a)
            diag = []
            for i in range(n_sub):
                q_i = qf[g][i * sb:(i + 1) * sb, :]
                b_i = b[g][i * sb:(i + 1) * sb, :]
                tile = jnp.zeros((sb, c), F32)
                for j in range(i * sb, (i + 1) * sb):
                    dec = jnp.exp2(b_i - b_ref[g, j:j + 1, :])
                    s = jnp.sum(q_i * kf_ref[g, j:j + 1, :] * dec, axis=-1, keepdims=True)
                    tile = jnp.where(lane_sb == j, s, tile)
                diag.append(jnp.where(lane_sb - i * sb <= row_sb, tile, 0.0))
            att.append((a + jnp.concatenate(diag, axis=0)).astype(BF16))

        for g in heads:
            b_last = b[g][c - 1:c, :]
            o[g] = o[g] + jnp.dot(att[g], v[g], preferred_element_type=F32)
            k_dec = (kf[g] * jnp.exp2(b_last - b[g])).astype(BF16)
            st_ref[g] = st[g] * jnp.exp2(b_last) + _dot_tn(v[g], k_dec)

        for g in heads:
            gate = r_ref[rows, vcols[g]].astype(F32)
            o_ref[rows, vcols[g]] = (_rms_scale(o[g]) * gn_ref[...] * (gate * _sigmoid(gate))).astype(o_ref.dtype)
        return carry

    lax.fori_loop(0, rows_per_step // c, chunk, 0)


def _gla(proj, small, wup_pad, b_up, g_norm, batch, seq, heads, dk, dv, col_q, col_k, col_v, col_r):
    group = min(GLA_GROUP, heads)
    ts = min(GLA_ROWS, seq)
    nt = seq // ts
    gk, gv = group * dk, group * dv
    cq0, ck0, cv0, cr0 = col_q // gk, col_k // gk, col_v // gv, col_r // gv
    blk = 2 * _nbytes((ts, gk), BF16) + 3 * _nbytes((ts, gv), BF16) + _nbytes((ts, LANES), F32)
    return pl.pallas_call(
        functools.partial(_gla_kernel, group=group, dk=dk, dv=dv),
        grid=(batch, heads // group, nt),
        in_specs=[
            pl.BlockSpec((ts, gk), lambda b, h, t: (b * nt + t, cq0 + h)),
            pl.BlockSpec((ts, gk), lambda b, h, t: (b * nt + t, ck0 + h)),
            pl.BlockSpec((ts, gv), lambda b, h, t: (b * nt + t, cv0 + h)),
            pl.BlockSpec((ts, gv), lambda b, h, t: (b * nt + t, cr0 + h)),
            pl.BlockSpec((ts, LANES), lambda b, h, t: (b * nt + t, 0)),
            pl.BlockSpec((LANES, gk), lambda b, h, t: (0, h)),
            pl.BlockSpec((1, gk), lambda b, h, t: (0, h)),
            pl.BlockSpec((1, dv), lambda b, h, t: (0, 0)),
        ],
        out_specs=pl.BlockSpec((ts, gv), lambda b, h, t: (b * nt + t, h)),
        out_shape=jax.ShapeDtypeStruct((batch * seq, heads * dv), BF16),
        scratch_shapes=[pltpu.VMEM((group, dv, dk), F32), pltpu.VMEM((group, GLA_CHUNK, dk), F32),
                        pltpu.VMEM((group, GLA_CHUNK, dk), F32), pltpu.VMEM((ts, gk), F32)],
        compiler_params=_params(("parallel", "parallel", "arbitrary"), blk),
        name="gla",
    )(proj, proj, proj, proj, small, wup_pad, b_up, g_norm)


def _merge_kernel(a_ref, wa_ref, f_ref, wf_ref, g0_ref, g1_ref, b0_ref, b1_ref, o_ref):
    ya = jnp.dot(a_ref[...], wa_ref[...], preferred_element_type=F32)
    yb = jnp.dot(f_ref[...], wf_ref[...], preferred_element_type=F32)
    g0 = _sigmoid(g0_ref[...].astype(F32) + b0_ref[...])
    g1 = _sigmoid(g1_ref[...].astype(F32) + b1_ref[...])
    o_ref[...] = (g0 * ya + g1 * yb).astype(o_ref.dtype)


def _merge(o_gla, w_a, o_fox, w_f, proj, b_gate, col_gates, bm=1024, bn=512):
    m, ka = o_gla.shape
    kf = o_fox.shape[1]
    d = w_a.shape[1]
    bm, bn = _pick_block(bm, m), _pick_block(bn, d, col_gates)
    g0, nb = col_gates // bn, d // bn
    blk = (_nbytes((bm, ka), BF16) + _nbytes((bm, kf), BF16) + _nbytes((ka, bn), BF16)
           + _nbytes((kf, bn), BF16) + 3 * _nbytes((bm, bn), BF16))
    return pl.pallas_call(
        _merge_kernel,
        grid=(m // bm, nb),
        in_specs=[
            pl.BlockSpec((bm, ka), lambda i, j: (i, 0)),
            pl.BlockSpec((ka, bn), lambda i, j: (0, j)),
            pl.BlockSpec((bm, kf), lambda i, j: (i, 0)),
            pl.BlockSpec((kf, bn), lambda i, j: (0, j)),
            pl.BlockSpec((bm, bn), lambda i, j: (i, g0 + j)),
            pl.BlockSpec((bm, bn), lambda i, j: (i, g0 + nb + j)),
            pl.BlockSpec((1, bn), lambda i, j: (0, j)),
            pl.BlockSpec((1, bn), lambda i, j: (0, nb + j)),
        ],
        out_specs=pl.BlockSpec((bm, bn), lambda i, j: (i, j)),
        out_shape=jax.ShapeDtypeStruct((m, d), BF16),
        compiler_params=_params(("parallel", "arbitrary"), blk),
        name="branch_merge",
    )(o_gla, w_a, o_fox, w_f, proj, proj, b_gate, b_gate)


def _xattn_kernel(q_ref, kv_ref, o_ref, *, heads, dh):
    for h in range(heads):
        q = q_ref[:, h * dh:(h + 1) * dh]
        k = kv_ref[:, h * dh:(h + 1) * dh]
        v = kv_ref[:, (heads + h) * dh:(heads + h + 1) * dh]
        s = _dot_nt(q, k) * (dh ** -0.5)
        p = jnp.exp(s - jnp.max(s, axis=-1, keepdims=True))
        l = jnp.sum(p, axis=-1, keepdims=True)
        o = jnp.dot(p.astype(BF16), v, preferred_element_type=F32) / l
        o_ref[:, h * dh:(h + 1) * dh] = o.astype(o_ref.dtype)


def _xattn(q, kv, batch, seq, mem_len, heads, dh, bq=512):
    nq = seq // bq
    w = heads * dh
    blk = 2 * _nbytes((bq, w), BF16) + _nbytes((mem_len, 2 * w), BF16) + 4 * _nbytes((bq, mem_len), F32)
    return pl.pallas_call(
        functools.partial(_xattn_kernel, heads=heads, dh=dh),
        grid=(batch, nq),
        in_specs=[pl.BlockSpec((bq, w), lambda b, i: (b * nq + i, 0)),
                  pl.BlockSpec((mem_len, 2 * w), lambda b, i: (b, 0))],
        out_specs=pl.BlockSpec((bq, w), lambda b, i: (b * nq + i, 0)),
        out_shape=jax.ShapeDtypeStruct((batch * seq, w), BF16),
        compiler_params=_params(("parallel", "arbitrary"), blk),
        name="cross_attention",
    )(q, kv)


def _ffn_up_kernel(a_ref, wg_ref, wu_ref, cg_ref, cu_ref, bg_ref, bu_ref, o_ref, w_scr, raw_ref,
                   *, blocks_per_seq):
    i = pl.program_id(1)
    bm = a_ref.shape[0]

    @pl.when(i == 0)
    def _():
        w_scr[0] = wg_ref[...].astype(BF16)
        w_scr[1] = wu_ref[...].astype(BF16)

    @pl.when(lax.rem(i, blocks_per_seq) == 0)
    def _():
        raw_ref[:, :SUBLANES, :] = jnp.zeros((2, SUBLANES, raw_ref.shape[2]), F32)

    a = a_ref[...]

    def conv(slot, c_ref, b_ref):
        raw_ref[slot, SUBLANES:, :] = jnp.dot(a, w_scr[slot], preferred_element_type=F32)
        x0 = raw_ref[slot, SUBLANES:, :]
        x1 = raw_ref[slot, SUBLANES - 1:SUBLANES - 1 + bm, :]
        x2 = raw_ref[slot, SUBLANES - 2:SUBLANES - 2 + bm, :]
        raw_ref[slot, :SUBLANES, :] = raw_ref[slot, bm:, :]
        return c_ref[0:1, :] * x2 + c_ref[1:2, :] * x1 + c_ref[2:3, :] * x0 + b_ref[...]

    g = conv(0, cg_ref, bg_ref)
    u = conv(1, cu_ref, bu_ref)
    gelu = 0.5 * g * (1.0 + jnp.tanh(GELU_C * (g + 0.044715 * (g * g * g))))
    o_ref[...] = (gelu * u).astype(o_ref.dtype)


def _ffn_up(h, w_up, w_conv, b_conv, seq, d_ff, bm=1024, bn=256):
    m, k = h.shape
    bm, bn = _pick_block(bm, seq), _pick_block(bn, d_ff)
    nb = d_ff // bn
    taps = w_conv.shape[0]
    wsp = lambda off: pl.BlockSpec((k, bn), lambda j, i: (0, off + j))
    csp = lambda off: pl.BlockSpec((taps, bn), lambda j, i: (0, off + j))
    bsp = lambda off: pl.BlockSpec((1, bn), lambda j, i: (0, off + j))
    scratch = 2 * _nbytes((k, bn), BF16) + 8 * _nbytes((bm, bn), F32)
    blk = _nbytes((bm, k), BF16) + 2 * _nbytes((k, bn), F32) + _nbytes((bm, bn), BF16) + scratch // 2
    return pl.pallas_call(
        functools.partial(_ffn_up_kernel, blocks_per_seq=seq // bm),
        grid=(nb, m // bm),
        in_specs=[pl.BlockSpec((bm, k), lambda j, i: (i, 0)), wsp(0), wsp(nb), csp(0), csp(nb), bsp(0), bsp(nb)],
        out_specs=pl.BlockSpec((bm, bn), lambda j, i: (i, j)),
        out_shape=jax.ShapeDtypeStruct((m, d_ff), BF16),
        scratch_shapes=[pltpu.VMEM((2, k, bn), BF16), pltpu.VMEM((2, SUBLANES + bm, bn), F32)],
        compiler_params=_params(("parallel", "arbitrary"), blk),
        name="ffn_up",
    )(h, w_up, w_up, w_conv, w_conv, b_conv, b_conv)


def _layer(x, mem, g_mix_pre, w_in, b_gate, w_gla_gate_up, b_gla_gate, g_gla_norm, b_fox_f,
           w_gla_branch, w_fox_branch, w_out, g_mix_post, g_xa_pre, g_mem, w_xa_q, w_xa_kv,
           w_xa_o, g_xa_post, g_ffn_pre, w_ffn_up, w_conv, b_conv, w_ffn_down, g_ffn_post,
           batch, seq):
    d_model = x.shape[1]
    gla_qk = w_gla_gate_up.shape[1]
    gla_dv = g_gla_norm.shape[0]
    gla_vw = w_gla_branch.shape[0]
    gla_heads = gla_vw // gla_dv
    gla_dk = gla_qk // gla_heads
    fox_w = w_fox_branch.shape[0]
    fox_heads = b_fox_f.shape[0]
    fox_dh = fox_w // fox_heads
    xa_w = w_xa_q.shape[1]
    mem_len = mem.shape[0] // batch
    xa_heads = XA_HEADS
    xa_dh = xa_w // xa_heads
    d_ff = w_ffn_down.shape[0]
    assert fox_heads + GLA_RANK <= LANES and gla_dk == LANES and fox_dh == LANES

    sizes = (gla_qk, gla_qk, gla_vw, gla_vw, GLA_RANK, fox_w, fox_w, fox_w, fox_heads, 2 * d_model)
    offs = [0]
    for s in sizes:
        offs.append(offs[-1] + s)
    o_gq, o_gk, o_gv, o_gr, o_low, o_fq, o_fk, o_fv, o_ff, o_gates = offs[:-1]
    wt = w_in.T.astype(BF16)
    wide_segments = [(o_gq, o_low - o_gq), (o_fq, o_ff - o_fq), (o_gates, 2 * d_model)]
    w_small = jnp.concatenate([wt[o_ff:o_ff + fox_heads], wt[o_low:o_low + GLA_RANK],
                               jnp.zeros((LANES - fox_heads - GLA_RANK, d_model), BF16)], axis=0)
    c_gq, c_gk, c_gv, c_gr = o_gq, o_gk, o_gv, o_gr
    c_fq = o_low
    c_fk, c_fv = c_fq + fox_w, c_fq + 2 * fox_w
    c_gates = c_fq + 3 * fox_w
    col_scale = jnp.ones((1, c_gates + 2 * d_model), F32).at[:, c_fq:c_fk].set(LOG2E * fox_dh ** -0.5)
    wup_pad = jnp.zeros((LANES, gla_qk), F32).at[fox_heads:fox_heads + GLA_RANK].set(w_gla_gate_up).astype(BF16)
    fox_bias_row = jnp.zeros((1, LANES), F32).at[0, :fox_heads].set(b_fox_f)

    h, small = _rmsnorm_side(x, g_mix_pre, w_small)
    proj = _in_proj(h, wt, wide_segments, col_scale, 1024, 1024, "in_proj")
    o_gla = _gla(proj, small, wup_pad, b_gla_gate.reshape(1, -1), g_gla_norm.reshape(1, -1),
                 batch, seq, gla_heads, gla_dk, gla_dv, c_gq, c_gk, c_gv, c_gr)
    cum_tok = _fox_bias(small, fox_bias_row, batch, seq)
    qx, kx, vt = _fox_prep(proj, cum_tok, batch, seq, fox_heads, fox_dh, c_fq, c_fk, c_fv)
    o_fox = _fox_attention(qx, kx, vt, batch, seq, fox_heads, fox_dh)
    merged = _merge(o_gla, w_gla_branch.astype(BF16), o_fox, w_fox_branch.astype(BF16), proj,
                    b_gate.reshape(1, -1), c_gates)
    x = _matmul_norm_resid(merged, w_out.astype(BF16), x, g_mix_post, 512, 512, "mix_out")

    q = _norm_matmul(x, g_xa_pre, w_xa_q.astype(BF16), BF16, 512, 1024, "xa_q")
    kv = _norm_matmul(mem, g_mem, w_xa_kv.astype(BF16), BF16, 512, 1024, "xa_kv")
    o = _xattn(q, kv, batch, seq, mem_len, xa_heads, xa_dh)
    x, h = _matmul_norm_resid(o, w_xa_o.astype(BF16), x, g_xa_post, 256, d_model, "xa_out",
                              next_gain=g_ffn_pre)

    act = _ffn_up(h, w_ffn_up, w_conv, b_conv.reshape(1, -1), seq, d_ff)
    y = _matmul(act, w_ffn_down.astype(BF16), F32, 512, 512, "ffn_down")
    return _norm_resid(y, x, g_ffn_post)


def kernel(x, mem, g_mix_pre, w_in, b_gate, w_gla_gate_up, b_gla_gate, g_gla_norm, b_fox_f, w_gla_branch, w_fox_branch, w_out, g_mix_post, g_xa_pre, g_mem, w_xa_q, w_xa_kv, w_xa_o, g_xa_post, g_ffn_pre, w_ffn_up, w_conv, b_conv, w_ffn_down, g_ffn_post):
    batch, seq, d_model = x.shape
    xf = x.reshape(batch * seq, d_model)
    mf = mem.reshape(-1, d_model)
    per_layer = (g_mix_pre, w_in, b_gate, w_gla_gate_up, b_gla_gate, g_gla_norm, b_fox_f, w_gla_branch,
                 w_fox_branch, w_out, g_mix_post, g_xa_pre, g_mem, w_xa_q, w_xa_kv, w_xa_o, g_xa_post,
                 g_ffn_pre, w_ffn_up, w_conv, b_conv, w_ffn_down, g_ffn_post)
    for l in range(w_in.shape[0]):
        xf = _layer(xf, mf, *(p[l] for p in per_layer), batch=batch, seq=seq)
    return xf.reshape(batch, seq, d_model)
```

```python
import functools
import math

import jax
import jax.numpy as jnp
from jax import lax
from jax.experimental import pallas as pl
from jax.experimental.pallas import tpu as pltpu

F32 = jnp.float32
BF16 = jnp.bfloat16

LANES = 128
SUBLANES = 8
BF16_SUBLANES = 16
V7X_VMEM_BYTES = 64 * 1024 * 1024
VMEM_CAP_BYTES = V7X_VMEM_BYTES - 8 * 1024 * 1024

EPS = 1e-6
GLA_TAU = 16.0
GLA_CHUNK = 64
GLA_SUB = 8
GLA_RANK = 16
GLA_GROUP = 8
GLA_ROWS = 1024
FOX_GROUP = 16
FOX_PREP_GROUP = 8
FOX_EXTRA = 6
XA_HEADS = 4
GELU_C = math.sqrt(2.0 / math.pi)
LOG2E = math.log2(math.e)
NEG_BIG = -1e30


def _params(semantics, block_bytes):
    limit = min(VMEM_CAP_BYTES, 2 * block_bytes + 16 * 1024 * 1024)
    return pltpu.CompilerParams(dimension_semantics=semantics, vmem_limit_bytes=int(limit))


def _nbytes(shape, dtype):
    return math.prod(shape) * jnp.dtype(dtype).itemsize


def _pick_block(target, *dims):
    g = math.gcd(*dims)
    best = None
    for b in range(LANES, min(target, g) + 1, LANES):
        if g % b == 0:
            best = b
    assert best is not None, (target, dims)
    return best


def _aligned_group(target, heads, width, *cols):
    group = target
    while heads % group or any(c % (group * width) for c in cols):
        group //= 2
    return group


def _log_sigmoid(z):
    return jnp.minimum(z, 0.0) - jnp.log(1.0 + jnp.exp(-jnp.abs(z)))


def _sigmoid(z):
    return 1.0 / (1.0 + jnp.exp(-z))


def _split3(x):
    hi = x.astype(BF16)
    r1 = x - hi.astype(F32)
    mid = r1.astype(BF16)
    lo = (r1 - mid.astype(F32)).astype(BF16)
    return hi, mid, lo


def _tril_cumsum(tril, x):
    hi, mid, lo = _split3(x)
    dot = functools.partial(jnp.dot, preferred_element_type=F32)
    return dot(tril, hi) + dot(tril, mid) + dot(tril, lo)


def _dot_nt(a, b):
    return lax.dot_general(a, b, (((1,), (1,)), ((), ())), preferred_element_type=F32)


def _dot_tn(a, b):
    return lax.dot_general(a, b, (((0,), (0,)), ((), ())), preferred_element_type=F32)


def _rms_scale(y):
    return y * lax.rsqrt(jnp.mean(y * y, axis=-1, keepdims=True) + EPS)


def _rmsnorm_side_kernel(x_ref, g_ref, wt_ref, o_ref, side_ref):
    h = (_rms_scale(x_ref[...]) * g_ref[...]).astype(o_ref.dtype)
    o_ref[...] = h
    side_ref[...] = _dot_nt(h, wt_ref[...])


def _rmsnorm_side(x, g, wt, bm=256):
    m, d = x.shape
    n = wt.shape[0]
    blk = _nbytes((bm, d), F32) + _nbytes((bm, d), BF16) + _nbytes((n, d), BF16) + _nbytes((bm, n), F32)
    return pl.pallas_call(
        _rmsnorm_side_kernel,
        grid=(m // bm,),
        in_specs=[pl.BlockSpec((bm, d), lambda i: (i, 0)), pl.BlockSpec((1, d), lambda i: (0, 0)),
                  pl.BlockSpec((n, d), lambda i: (0, 0))],
        out_specs=[pl.BlockSpec((bm, d), lambda i: (i, 0)), pl.BlockSpec((bm, n), lambda i: (i, 0))],
        out_shape=[jax.ShapeDtypeStruct((m, d), BF16), jax.ShapeDtypeStruct((m, n), F32)],
        compiler_params=_params(("parallel",), blk),
        name="rmsnorm",
    )(x, g.reshape(1, d), wt)


def _norm_resid_kernel(y_ref, x_ref, g_ref, o_ref):
    o_ref[...] = x_ref[...] + _rms_scale(y_ref[...]) * g_ref[...]


def _norm_resid(y, x, g, bm=256):
    m, d = x.shape
    blk = 3 * _nbytes((bm, d), F32)
    row = pl.BlockSpec((bm, d), lambda i: (i, 0))
    return pl.pallas_call(
        _norm_resid_kernel,
        grid=(m // bm,),
        in_specs=[row, row, pl.BlockSpec((1, d), lambda i: (0, 0))],
        out_specs=row,
        out_shape=jax.ShapeDtypeStruct((m, d), F32),
        compiler_params=_params(("parallel",), blk),
        name="norm_resid",
    )(y, x, g.reshape(1, d))


def _mm_kernel(a_ref, w_ref, o_ref):
    o_ref[...] = jnp.dot(a_ref[...], w_ref[...], preferred_element_type=F32).astype(o_ref.dtype)


def _matmul(a, w, out_dtype, bm, bn, name):
    m, k = a.shape
    n = w.shape[1]
    bm, bn = _pick_block(bm, m), _pick_block(bn, n)
    blk = _nbytes((bm, k), a.dtype) + _nbytes((k, bn), w.dtype) + _nbytes((bm, bn), out_dtype)
    return pl.pallas_call(
        _mm_kernel,
        grid=(m // bm, n // bn),
        in_specs=[pl.BlockSpec((bm, k), lambda i, j: (i, 0)), pl.BlockSpec((k, bn), lambda i, j: (0, j))],
        out_specs=pl.BlockSpec((bm, bn), lambda i, j: (i, j)),
        out_shape=jax.ShapeDtypeStruct((m, n), out_dtype),
        compiler_params=_params(("parallel", "arbitrary"), blk),
        name=name,
    )(a, w)


def _in_proj_kernel(row_tab, a_ref, w_ref, s_ref, o_ref):
    o_ref[...] = (_dot_nt(a_ref[...], w_ref[...]) * s_ref[...]).astype(o_ref.dtype)


def _in_proj(a, wt, segments, col_scale, bm, bn, name):
    m, k = a.shape
    bm = _pick_block(bm, m)
    bn = _pick_block(bn, *(rows for _, rows in segments))
    n = sum(rows for _, rows in segments)
    firsts = [first_row + r for first_row, rows in segments for r in range(0, rows, bn)]
    assert all(f % BF16_SUBLANES == 0 for f in firsts)
    row_tab = jnp.asarray([f // BF16_SUBLANES for f in firsts], jnp.int32)

    grid_spec = pltpu.PrefetchScalarGridSpec(
        num_scalar_prefetch=1,
        grid=(m // bm, n // bn),
        in_specs=[pl.BlockSpec((bm, k), lambda i, j, tab: (i, 0)),
                  pl.BlockSpec((pl.Element(bn), pl.Element(k)), lambda i, j, tab: (tab[j] * BF16_SUBLANES, 0)),
                  pl.BlockSpec((1, bn), lambda i, j, tab: (0, j))],
        out_specs=pl.BlockSpec((bm, bn), lambda i, j, tab: (i, j)),
    )
    blk = _nbytes((bm, k), a.dtype) + _nbytes((bn, k), wt.dtype) + _nbytes((bm, bn), BF16)
    return pl.pallas_call(
        _in_proj_kernel,
        grid_spec=grid_spec,
        out_shape=jax.ShapeDtypeStruct((m, n), BF16),
        compiler_params=_params(("parallel", "arbitrary"), blk),
        name=name,
    )(row_tab, a, wt, col_scale)


def _norm_mm_kernel(x_ref, g_ref, w_ref, o_ref, h_scr):
    @pl.when(pl.program_id(1) == 0)
    def _():
        h_scr[...] = (_rms_scale(x_ref[...]) * g_ref[...]).astype(h_scr.dtype)

    o_ref[...] = jnp.dot(h_scr[...], w_ref[...], preferred_element_type=F32).astype(o_ref.dtype)


def _norm_matmul(x, g, w, out_dtype, bm, bn, name):
    m, k = x.shape
    n = w.shape[1]
    bm, bn = _pick_block(bm, m), _pick_block(bn, n)
    blk = (_nbytes((bm, k), F32) + _nbytes((k, bn), w.dtype) + _nbytes((bm, bn), out_dtype)
           + _nbytes((bm, k), BF16) // 2)
    return pl.pallas_call(
        _norm_mm_kernel,
        grid=(m // bm, n // bn),
        in_specs=[pl.BlockSpec((bm, k), lambda i, j: (i, 0)), pl.BlockSpec((1, k), lambda i, j: (0, 0)),
                  pl.BlockSpec((k, bn), lambda i, j: (0, j))],
        out_specs=pl.BlockSpec((bm, bn), lambda i, j: (i, j)),
        out_shape=jax.ShapeDtypeStruct((m, n), out_dtype),
        scratch_shapes=[pltpu.VMEM((bm, k), BF16)],
        compiler_params=_params(("parallel", "arbitrary"), blk),
        name=name,
    )(x, g.reshape(1, k), w)


def _mm_norm_resid_kernel(a_ref, w_ref, x_ref, g_ref, *rest):
    o_ref = rest[-2] if len(rest) == 3 else rest[0]
    j = pl.program_id(1)
    bn = w_ref.shape[1]
    col = pl.multiple_of(j * bn, bn)
    o_ref[:, pl.ds(col, bn)] = jnp.dot(a_ref[...], w_ref[...], preferred_element_type=F32)

    @pl.when(j == pl.num_programs(1) - 1)
    def _():
        y = x_ref[...] + _rms_scale(o_ref[...]) * g_ref[...]
        o_ref[...] = y
        if len(rest) == 3:
            next_gain_ref, _, h_ref = rest
            h_ref[...] = (_rms_scale(y) * next_gain_ref[...]).astype(h_ref.dtype)


def _matmul_norm_resid(a, w, x, g, bm, bn, name, next_gain=None):
    m, k = a.shape
    n = w.shape[1]
    bm, bn = _pick_block(bm, m), _pick_block(bn, n)
    row = pl.BlockSpec((bm, n), lambda i, j: (i, 0))
    vec = pl.BlockSpec((1, n), lambda i, j: (0, 0))
    in_specs = [pl.BlockSpec((bm, k), lambda i, j: (i, 0)), pl.BlockSpec((k, bn), lambda i, j: (0, j)), row, vec]
    args = [a, w, x, g.reshape(1, n)]
    out_specs, out_shape = row, jax.ShapeDtypeStruct((m, n), F32)
    blk = _nbytes((bm, k), a.dtype) + _nbytes((k, bn), w.dtype) + 2 * _nbytes((bm, n), F32)
    if next_gain is not None:
        in_specs.append(vec)
        args.append(next_gain.reshape(1, n))
        out_specs, out_shape = [row, row], [out_shape, jax.ShapeDtypeStruct((m, n), BF16)]
        blk += _nbytes((bm, n), BF16)
    return pl.pallas_call(
        _mm_norm_resid_kernel,
        grid=(m // bm, n // bn),
        in_specs=in_specs,
        out_specs=out_specs,
        out_shape=out_shape,
        compiler_params=_params(("parallel", "arbitrary"), blk),
        name=name,
    )(*args)


def _fox_bias_kernel(sm_ref, bias_ref, ct_ref):
    blk = LANES
    n_blk = sm_ref.shape[0] // blk
    r_i = lax.broadcasted_iota(jnp.int32, (blk, blk), 0)
    c_i = lax.broadcasted_iota(jnp.int32, (blk, blk), 1)
    tril = (r_i >= c_i).astype(BF16)

    def body(i, carry):
        start = pl.multiple_of(i * blk, blk)
        z = sm_ref[pl.ds(start, blk), :] + bias_ref[...]
        cs = _tril_cumsum(tril, _log_sigmoid(z)) + carry
        ct_ref[pl.ds(start, blk), :] = cs
        return cs[blk - 1:blk, :]

    lax.fori_loop(0, n_blk, body, jnp.zeros((1, blk), F32))


def _fox_bias(small, bias_row, batch, seq):
    tok = pl.BlockSpec((seq, LANES), lambda b: (b, 0))
    return pl.pallas_call(
        _fox_bias_kernel,
        grid=(batch,),
        in_specs=[tok, pl.BlockSpec((1, LANES), lambda b: (0, 0))],
        out_specs=tok,
        out_shape=jax.ShapeDtypeStruct((batch * seq, LANES), F32),
        compiler_params=_params(("parallel",), 2 * _nbytes((seq, LANES), F32)),
        name="fox_bias",
    )(small, bias_row)


def _fox_prep_kernel(q_ref, k_ref, v_ref, cum_ref, qx_ref, kx_ref, vt_ref, *, group, dh):
    rows = q_ref.shape[0]
    lane_c = lax.broadcasted_iota(jnp.int32, cum_ref.shape, 1)
    lane = lax.broadcasted_iota(jnp.int32, (rows, dh), 1)
    cum = cum_ref[...]
    for g in range(group):
        head = pl.program_id(1) * group + g
        c = jnp.sum(jnp.where(lane_c == head, cum, 0.0), axis=1, keepdims=True) * LOG2E
        hi, mid, lo = (t.astype(F32) for t in _split3(c))
        eq = jnp.where(lane == 0, hi, jnp.where(lane == 1, mid, jnp.where(lane == 2, lo,
                       jnp.where(lane < FOX_EXTRA, 1.0, 0.0))))
        ek = jnp.where(lane < 3, 1.0, jnp.where(lane == 3, -hi, jnp.where(lane == 4, -mid,
                       jnp.where(lane == 5, -lo, 0.0))))
        src = slice(g * dh, (g + 1) * dh)
        qx_ref[:, 2 * g * dh:(2 * g + 1) * dh] = q_ref[:, src]
        qx_ref[:, (2 * g + 1) * dh:(2 * g + 2) * dh] = eq.astype(BF16)
        kx_ref[:, 2 * g * dh:(2 * g + 1) * dh] = k_ref[:, src]
        kx_ref[:, (2 * g + 1) * dh:(2 * g + 2) * dh] = ek.astype(BF16)
        vt_ref[g, :dh, :] = v_ref[:, src].astype(F32).T.astype(BF16)
        vt_ref[g, dh:, :] = jnp.ones((vt_ref.shape[1] - dh, rows), BF16)


def _fox_prep(proj, cum_tok, batch, seq, heads, dh, col_q, col_k, col_v, rows=1024):
    t = proj.shape[0]
    group = _aligned_group(FOX_PREP_GROUP, heads, dh, col_q, col_k, col_v)
    rows = _pick_block(rows, seq)
    per_seq = seq // rows
    gd = group * dh
    cq0, ck0, cv0 = col_q // gd, col_k // gd, col_v // gd
    aug = jax.ShapeDtypeStruct((t, heads * 2 * dh), BF16)
    vrows = dh + BF16_SUBLANES
    blk = (3 * _nbytes((rows, gd), BF16) + _nbytes((rows, LANES), F32) + 2 * _nbytes((rows, 2 * gd), BF16)
           + _nbytes((group, vrows, rows), BF16) + 4 * _nbytes((rows, dh), F32))
    col = lambda c0: pl.BlockSpec((rows, gd), lambda i, h: (i, c0 + h))
    return pl.pallas_call(
        functools.partial(_fox_prep_kernel, group=group, dh=dh),
        grid=(t // rows, heads // group),
        in_specs=[col(cq0), col(ck0), col(cv0), pl.BlockSpec((rows, LANES), lambda i, h: (i, 0))],
        out_specs=[pl.BlockSpec((rows, 2 * gd), lambda i, h: (i, h)),
                   pl.BlockSpec((rows, 2 * gd), lambda i, h: (i, h)),
                   pl.BlockSpec((None, group, vrows, rows), lambda i, h: (i // per_seq, h, 0, i % per_seq))],
        out_shape=[aug, aug, jax.ShapeDtypeStruct((batch, heads, vrows, seq), BF16)],
        compiler_params=_params(("parallel", "arbitrary"), blk),
        name="fox_prep",
    )(proj, proj, proj, cum_tok)


def _fox_kernel(qi_tab, kj_tab, qx_ref, kx_ref, vt_ref, o_ref, m_scr, acc_scr, *, group, dh):
    step = pl.program_id(2)
    qi = qi_tab[step]
    kj = kj_tab[step]
    bq = qx_ref.shape[0]
    bk = kx_ref.shape[0]
    dx = 2 * dh

    @pl.when(kj == 0)
    def _():
        m_scr[...] = jnp.full_like(m_scr, NEG_BIG)
        acc_scr[...] = jnp.zeros_like(acc_scr)

    def update(masked):
        scores = [_dot_nt(kx_ref[:, g * dx:(g + 1) * dx], qx_ref[:, g * dx:(g + 1) * dx])
                  for g in range(group)]
        for g in range(group):
            z = scores[g]
            if masked:
                key = lax.broadcasted_iota(jnp.int32, (bk, bq), 0)
                qry = lax.broadcasted_iota(jnp.int32, (bk, bq), 1)
                z = jnp.where(key <= qry, z, NEG_BIG)
            m_prev = m_scr[g]
            m_new = jnp.maximum(m_prev, jnp.max(z, axis=0, keepdims=True))
            p = jnp.exp2(z - m_new).astype(BF16)
            acc_scr[g] = (jnp.exp2(m_prev - m_new) * acc_scr[g]
                          + jnp.dot(vt_ref[g], p, preferred_element_type=F32))
            m_scr[g] = m_new

    @pl.when(kj != qi)
    def _():
        update(False)

    @pl.when(kj == qi)
    def _():
        update(True)
        for g in range(group):
            acc = acc_scr[g]
            o_ref[:, g * dh:(g + 1) * dh] = (acc[:dh] / acc[dh:dh + 1]).T.astype(o_ref.dtype)


def _fox_attention(qx, kx, vt, batch, seq, heads, dh, bq=512):
    group = _aligned_group(FOX_GROUP, heads, dh)
    nq = seq // bq
    pairs = [(i, j) for i in range(nq) for j in range(i + 1)]
    qi_tab = jnp.asarray([p[0] for p in pairs], jnp.int32)
    kj_tab = jnp.asarray([p[1] for p in pairs], jnp.int32)
    dxg = group * 2 * dh
    vrows = vt.shape[2]

    grid_spec = pltpu.PrefetchScalarGridSpec(
        num_scalar_prefetch=2,
        grid=(batch, heads // group, len(pairs)),
        in_specs=[
            pl.BlockSpec((bq, dxg), lambda b, h, s, qt, kt: (b * nq + qt[s], h)),
            pl.BlockSpec((bq, dxg), lambda b, h, s, qt, kt: (b * nq + kt[s], h)),
            pl.BlockSpec((None, group, vrows, bq), lambda b, h, s, qt, kt: (b, h, 0, kt[s])),
        ],
        out_specs=pl.BlockSpec((bq, group * dh), lambda b, h, s, qt, kt: (b * nq + qt[s], h)),
        scratch_shapes=[pltpu.VMEM((group, 1, bq), F32), pltpu.VMEM((group, vrows, bq), F32)],
    )
    blk = (2 * _nbytes((bq, dxg), BF16) + _nbytes((group, vrows, bq), BF16) + _nbytes((bq, group * dh), BF16)
           + 6 * _nbytes((bq, bq), F32))
    return pl.pallas_call(
        functools.partial(_fox_kernel, group=group, dh=dh),
        grid_spec=grid_spec,
        out_shape=jax.ShapeDtypeStruct((batch * seq, heads * dh), BF16),
        compiler_params=_params(("parallel", "parallel", "arbitrary"), blk),
        name="fox_attention",
    )(qi_tab, kj_tab, qx, kx, vt)


def _gla_kernel(q_ref, k_ref, v_ref, r_ref, low_ref, wup_ref, bup_ref, gn_ref, o_ref, st_ref, kf_ref, b_ref,
                ball_ref, *, group, dk, dv):
    c, sb = GLA_CHUNK, GLA_SUB
    n_sub = c // sb
    rows_per_step = q_ref.shape[0]
    scale = dk ** -0.5

    @pl.when(pl.program_id(2) == 0)
    def _():
        st_ref[...] = jnp.zeros_like(st_ref)

    r_i = lax.broadcasted_iota(jnp.int32, (c, c), 0)
    c_i = lax.broadcasted_iota(jnp.int32, (c, c), 1)
    tril = (r_i >= c_i).astype(BF16)
    sub_shift = sb.bit_length() - 1
    sub_row = jnp.right_shift(r_i, sub_shift)
    sub_col = jnp.right_shift(c_i, sub_shift)
    lane_sb = lax.broadcasted_iota(jnp.int32, (sb, c), 1)
    row_sb = lax.broadcasted_iota(jnp.int32, (sb, c), 0)

    heads = range(group)
    kcols = [slice(g * dk, (g + 1) * dk) for g in heads]
    vcols = [slice(g * dv, (g + 1) * dv) for g in heads]

    z_all = jnp.dot(low_ref[...].astype(BF16), wup_ref[...], preferred_element_type=F32) + bup_ref[...]
    terms = _split3(_log_sigmoid(z_all) * (LOG2E / GLA_TAU))
    for ci in range(rows_per_step // c):
        rows = slice(ci * c, (ci + 1) * c)
        hi, mid, lo = (jnp.dot(tril, t[rows, :], preferred_element_type=F32) for t in terms)
        ball_ref[rows, :] = hi + mid + lo

    def chunk(ci, carry):
        rows = pl.ds(pl.multiple_of(ci * c, c), c)
        b = [ball_ref[rows, kcols[g]] for g in heads]
        qf = [q_ref[rows, kcols[g]].astype(F32) * scale for g in heads]
        kf = [k_ref[rows, kcols[g]].astype(F32) for g in heads]
        v = [v_ref[rows, vcols[g]] for g in heads]
        st = [st_ref[g] for g in heads]

        o = [_dot_nt((qf[g] * jnp.exp2(b[g])).astype(BF16), st[g].astype(BF16)) for g in heads]

        r = []
        for g in heads:
            kf_ref[g] = kf[g]
            b_ref[g] = b[g]
            ends = [jnp.broadcast_to(b_ref[g, (j + 1) * sb - 1:(j + 1) * sb, :], (c, dk)) for j in range(n_sub)]
            end_of_row = jnp.concatenate([e[:sb] for e in ends], axis=0)
            k_hat = (kf[g] * jnp.exp2(end_of_row - b[g])).astype(BF16)
            q_stack = jnp.concatenate(
                [(qf[g] * jnp.exp2(b[g] - ends[j])).astype(BF16) for j in range(n_sub - 1)],
                axis=0)
            r.append(_dot_nt(q_stack, k_hat))

        att = []
        for g in heads:
            a = jnp.zeros((c, c), F32)
            for j in range(n_sub - 1):
                a = jnp.where((sub_col == j) & (sub_row > j), r[g][j * c:(j + 1) * c, :], a)
            diag = []
            for i in range(n_sub):
                q_i = qf[g][i * sb:(i + 1) * sb, :]
                b_i = b[g][i * sb:(i + 1) * sb, :]
                tile = jnp.zeros((sb, c), F32)
                for j in range(i * sb, (i + 1) * sb):
                    dec = jnp.exp2(b_i - b_ref[g, j:j + 1, :])
                    s = jnp.sum(q_i * kf_ref[g, j:j + 1, :] * dec, axis=-1, keepdims=True)
                    tile = jnp.where(lane_sb == j, s, tile)
                diag.append(jnp.where(lane_sb - i * sb <= row_sb, tile, 0.0))
            att.append((a + jnp.concatenate(diag, axis=0)).astype(BF16))

        for g in heads:
            b_last = b[g][c - 1:c, :]
            o[g] = o[g] + jnp.dot(att[g], v[g], preferred_element_type=F32)
            k_dec = (kf[g] * jnp.exp2(b_last - b[g])).astype(BF16)
            st_ref[g] = st[g] * jnp.exp2(b_last) + _dot_tn(v[g], k_dec)

        for g in heads:
            gate = r_ref[rows, vcols[g]].astype(F32)
            o_ref[rows, vcols[g]] = (_rms_scale(o[g]) * gn_ref[...] * (gate * _sigmoid(gate))).astype(o_ref.dtype)
        return carry

    lax.fori_loop(0, rows_per_step // c, chunk, 0)


def _gla(proj, small, wup_pad, b_up, g_norm, batch, seq, heads, dk, dv, col_q, col_k, col_v, col_r):
    group = _aligned_group(GLA_GROUP, heads, dk, col_q, col_k)
    group = _aligned_group(group, heads, dv, col_v, col_r)
    ts = min(GLA_ROWS, seq)
    nt = seq // ts
    gk, gv = group * dk, group * dv
    cq0, ck0, cv0, cr0 = col_q // gk, col_k // gk, col_v // gv, col_r // gv
    blk = 2 * _nbytes((ts, gk), BF16) + 3 * _nbytes((ts, gv), BF16) + _nbytes((ts, LANES), F32)
    return pl.pallas_call(
        functools.partial(_gla_kernel, group=group, dk=dk, dv=dv),
        grid=(batch, heads // group, nt),
        in_specs=[
            pl.BlockSpec((ts, gk), lambda b, h, t: (b * nt + t, cq0 + h)),
            pl.BlockSpec((ts, gk), lambda b, h, t: (b * nt + t, ck0 + h)),
            pl.BlockSpec((ts, gv), lambda b, h, t: (b * nt + t, cv0 + h)),
            pl.BlockSpec((ts, gv), lambda b, h, t: (b * nt + t, cr0 + h)),
            pl.BlockSpec((ts, LANES), lambda b, h, t: (b * nt + t, 0)),
            pl.BlockSpec((LANES, gk), lambda b, h, t: (0, h)),
            pl.BlockSpec((1, gk), lambda b, h, t: (0, h)),
            pl.BlockSpec((1, dv), lambda b, h, t: (0, 0)),
        ],
        out_specs=pl.BlockSpec((ts, gv), lambda b, h, t: (b * nt + t, h)),
        out_shape=jax.ShapeDtypeStruct((batch * seq, heads * dv), BF16),
        scratch_shapes=[pltpu.VMEM((group, dv, dk), F32), pltpu.VMEM((group, GLA_CHUNK, dk), F32),
                        pltpu.VMEM((group, GLA_CHUNK, dk), F32), pltpu.VMEM((ts, gk), F32)],
        compiler_params=_params(("parallel", "parallel", "arbitrary"), blk),
        name="gla",
    )(proj, proj, proj, proj, small, wup_pad, b_up, g_norm)


def _merge_kernel(a_ref, wa_ref, f_ref, wf_ref, g0_ref, g1_ref, b0_ref, b1_ref, o_ref):
    ya = jnp.dot(a_ref[...], wa_ref[...], preferred_element_type=F32)
    yb = jnp.dot(f_ref[...], wf_ref[...], preferred_element_type=F32)
    g0 = _sigmoid(g0_ref[...].astype(F32) + b0_ref[...])
    g1 = _sigmoid(g1_ref[...].astype(F32) + b1_ref[...])
    o_ref[...] = (g0 * ya + g1 * yb).astype(o_ref.dtype)


def _merge(o_gla, w_a, o_fox, w_f, proj, b_gate, col_gates, bm=1024, bn=1024):
    m, ka = o_gla.shape
    kf = o_fox.shape[1]
    d = w_a.shape[1]
    bm, bn = _pick_block(bm, m), _pick_block(bn, d, col_gates)
    g0, nb = col_gates // bn, d // bn
    blk = (_nbytes((bm, ka), BF16) + _nbytes((bm, kf), BF16) + _nbytes((ka, bn), BF16)
           + _nbytes((kf, bn), BF16) + 3 * _nbytes((bm, bn), BF16))
    return pl.pallas_call(
        _merge_kernel,
        grid=(m // bm, nb),
        in_specs=[
            pl.BlockSpec((bm, ka), lambda i, j: (i, 0)),
            pl.BlockSpec((ka, bn), lambda i, j: (0, j)),
            pl.BlockSpec((bm, kf), lambda i, j: (i, 0)),
            pl.BlockSpec((kf, bn), lambda i, j: (0, j)),
            pl.BlockSpec((bm, bn), lambda i, j: (i, g0 + j)),
            pl.BlockSpec((bm, bn), lambda i, j: (i, g0 + nb + j)),
            pl.BlockSpec((1, bn), lambda i, j: (0, j)),
            pl.BlockSpec((1, bn), lambda i, j: (0, nb + j)),
        ],
        out_specs=pl.BlockSpec((bm, bn), lambda i, j: (i, j)),
        out_shape=jax.ShapeDtypeStruct((m, d), BF16),
        compiler_params=_params(("parallel", "arbitrary"), blk),
        name="branch_merge",
    )(o_gla, w_a, o_fox, w_f, proj, proj, b_gate, b_gate)


def _xattn_kernel(q_ref, kv_ref, o_ref, *, heads, dh):
    for h in range(heads):
        q = q_ref[:, h * dh:(h + 1) * dh]
        k = kv_ref[:, h * dh:(h + 1) * dh]
        v = kv_ref[:, (heads + h) * dh:(heads + h + 1) * dh]
        s = _dot_nt(q, k) * (dh ** -0.5)
        p = jnp.exp(s - jnp.max(s, axis=-1, keepdims=True))
        l = jnp.sum(p, axis=-1, keepdims=True)
        o = jnp.dot(p.astype(BF16), v, preferred_element_type=F32) / l
        o_ref[:, h * dh:(h + 1) * dh] = o.astype(o_ref.dtype)


def _xattn(q, kv, batch, seq, mem_len, heads, dh, bq=512):
    nq = seq // bq
    w = heads * dh
    blk = 2 * _nbytes((bq, w), BF16) + _nbytes((mem_len, 2 * w), BF16) + 4 * _nbytes((bq, mem_len), F32)
    return pl.pallas_call(
        functools.partial(_xattn_kernel, heads=heads, dh=dh),
        grid=(batch, nq),
        in_specs=[pl.BlockSpec((bq, w), lambda b, i: (b * nq + i, 0)),
                  pl.BlockSpec((mem_len, 2 * w), lambda b, i: (b, 0))],
        out_specs=pl.BlockSpec((bq, w), lambda b, i: (b * nq + i, 0)),
        out_shape=jax.ShapeDtypeStruct((batch * seq, w), BF16),
        compiler_params=_params(("parallel", "arbitrary"), blk),
        name="cross_attention",
    )(q, kv)


def _ffn_up_kernel(a_ref, wg_ref, wu_ref, cg_ref, cu_ref, bg_ref, bu_ref, o_ref, w_scr, raw_ref,
                   *, blocks_per_seq):
    i = pl.program_id(1)
    bm = a_ref.shape[0]

    @pl.when(i == 0)
    def _():
        w_scr[0] = wg_ref[...].astype(BF16)
        w_scr[1] = wu_ref[...].astype(BF16)

    @pl.when(lax.rem(i, blocks_per_seq) == 0)
    def _():
        raw_ref[:, :SUBLANES, :] = jnp.zeros((2, SUBLANES, raw_ref.shape[2]), F32)

    a = a_ref[...]

    def conv(slot, c_ref, b_ref):
        raw_ref[slot, SUBLANES:, :] = jnp.dot(a, w_scr[slot], preferred_element_type=F32)
        x0 = raw_ref[slot, SUBLANES:, :]
        x1 = raw_ref[slot, SUBLANES - 1:SUBLANES - 1 + bm, :]
        x2 = raw_ref[slot, SUBLANES - 2:SUBLANES - 2 + bm, :]
        raw_ref[slot, :SUBLANES, :] = raw_ref[slot, bm:, :]
        return c_ref[0:1, :] * x2 + c_ref[1:2, :] * x1 + c_ref[2:3, :] * x0 + b_ref[...]

    g = conv(0, cg_ref, bg_ref)
    u = conv(1, cu_ref, bu_ref)
    gelu = (0.5 * g) * (1.0 + jnp.tanh(g * (GELU_C + (GELU_C * 0.044715) * (g * g))))
    o_ref[...] = (gelu * u).astype(o_ref.dtype)


def _ffn_up(h, w_up, w_conv, b_conv, seq, d_ff, bm=1024, bn=256):
    m, k = h.shape
    bm, bn = _pick_block(bm, seq), _pick_block(bn, d_ff)
    nb = d_ff // bn
    taps = w_conv.shape[0]
    wsp = lambda off: pl.BlockSpec((k, bn), lambda j, i: (0, off + j))
    csp = lambda off: pl.BlockSpec((taps, bn), lambda j, i: (0, off + j))
    bsp = lambda off: pl.BlockSpec((1, bn), lambda j, i: (0, off + j))
    scratch = 2 * _nbytes((k, bn), BF16) + 8 * _nbytes((bm, bn), F32)
    blk = _nbytes((bm, k), BF16) + 2 * _nbytes((k, bn), F32) + _nbytes((bm, bn), BF16) + scratch // 2
    return pl.pallas_call(
        functools.partial(_ffn_up_kernel, blocks_per_seq=seq // bm),
        grid=(nb, m // bm),
        in_specs=[pl.BlockSpec((bm, k), lambda j, i: (i, 0)), wsp(0), wsp(nb), csp(0), csp(nb), bsp(0), bsp(nb)],
        out_specs=pl.BlockSpec((bm, bn), lambda j, i: (i, j)),
        out_shape=jax.ShapeDtypeStruct((m, d_ff), BF16),
        scratch_shapes=[pltpu.VMEM((2, k, bn), BF16), pltpu.VMEM((2, SUBLANES + bm, bn), F32)],
        compiler_params=_params(("parallel", "arbitrary"), blk),
        name="ffn_up",
    )(h, w_up, w_up, w_conv, w_conv, b_conv, b_conv)


def _layer(x, mem, g_mix_pre, w_in, b_gate, w_gla_gate_up, b_gla_gate, g_gla_norm, b_fox_f,
           w_gla_branch, w_fox_branch, w_out, g_mix_post, g_xa_pre, g_mem, w_xa_q, w_xa_kv,
           w_xa_o, g_xa_post, g_ffn_pre, w_ffn_up, w_conv, b_conv, w_ffn_down, g_ffn_post,
           batch, seq):
    d_model = x.shape[1]
    gla_qk = w_gla_gate_up.shape[1]
    gla_dv = g_gla_norm.shape[0]
    gla_vw = w_gla_branch.shape[0]
    gla_heads = gla_vw // gla_dv
    gla_dk = gla_qk // gla_heads
    fox_w = w_fox_branch.shape[0]
    fox_heads = b_fox_f.shape[0]
    fox_dh = fox_w // fox_heads
    xa_w = w_xa_q.shape[1]
    mem_len = mem.shape[0] // batch
    xa_heads = XA_HEADS
    xa_dh = xa_w // xa_heads
    d_ff = w_ffn_down.shape[0]
    assert fox_heads + GLA_RANK <= LANES and gla_dk == LANES and fox_dh == LANES

    sizes = (gla_qk, gla_qk, gla_vw, gla_vw, GLA_RANK, fox_w, fox_w, fox_w, fox_heads, 2 * d_model)
    offs = [0]
    for s in sizes:
        offs.append(offs[-1] + s)
    o_gq, o_gk, o_gv, o_gr, o_low, o_fq, o_fk, o_fv, o_ff, o_gates = offs[:-1]
    wt = w_in.T.astype(BF16)
    wide_segments = [(o_gq, o_low - o_gq), (o_fq, o_ff - o_fq), (o_gates, 2 * d_model)]
    w_small = jnp.concatenate([wt[o_ff:o_ff + fox_heads], wt[o_low:o_low + GLA_RANK],
                               jnp.zeros((LANES - fox_heads - GLA_RANK, d_model), BF16)], axis=0)
    c_gq, c_gk, c_gv, c_gr = o_gq, o_gk, o_gv, o_gr
    c_fq = o_low
    c_fk, c_fv = c_fq + fox_w, c_fq + 2 * fox_w
    c_gates = c_fq + 3 * fox_w
    col_scale = jnp.ones((1, c_gates + 2 * d_model), F32).at[:, c_fq:c_fk].set(LOG2E * fox_dh ** -0.5)
    wup_pad = jnp.zeros((LANES, gla_qk), F32).at[fox_heads:fox_heads + GLA_RANK].set(w_gla_gate_up).astype(BF16)
    fox_bias_row = jnp.zeros((1, LANES), F32).at[0, :fox_heads].set(b_fox_f)

    h, small = _rmsnorm_side(x, g_mix_pre, w_small)
    proj = _in_proj(h, wt, wide_segments, col_scale, 1024, 1024, "in_proj")
    o_gla = _gla(proj, small, wup_pad, b_gla_gate.reshape(1, -1), g_gla_norm.reshape(1, -1),
                 batch, seq, gla_heads, gla_dk, gla_dv, c_gq, c_gk, c_gv, c_gr)
    cum_tok = _fox_bias(small, fox_bias_row, batch, seq)
    qx, kx, vt = _fox_prep(proj, cum_tok, batch, seq, fox_heads, fox_dh, c_fq, c_fk, c_fv)
    o_fox = _fox_attention(qx, kx, vt, batch, seq, fox_heads, fox_dh)
    merged = _merge(o_gla, w_gla_branch.astype(BF16), o_fox, w_fox_branch.astype(BF16), proj,
                    b_gate.reshape(1, -1), c_gates)
    x = _matmul_norm_resid(merged, w_out.astype(BF16), x, g_mix_post, 512, 512, "mix_out")

    q = _norm_matmul(x, g_xa_pre, w_xa_q.astype(BF16), BF16, 512, 1024, "xa_q")
    kv = _norm_matmul(mem, g_mem, w_xa_kv.astype(BF16), BF16, 512, 1024, "xa_kv")
    o = _xattn(q, kv, batch, seq, mem_len, xa_heads, xa_dh)
    x, h = _matmul_norm_resid(o, w_xa_o.astype(BF16), x, g_xa_post, 256, d_model, "xa_out",
                              next_gain=g_ffn_pre)

    act = _ffn_up(h, w_ffn_up, w_conv, b_conv.reshape(1, -1), seq, d_ff)
    y = _matmul(act, w_ffn_down.astype(BF16), F32, 512, 512, "ffn_down")
    return _norm_resid(y, x, g_ffn_post)


def kernel(x, mem, g_mix_pre, w_in, b_gate, w_gla_gate_up, b_gla_gate, g_gla_norm, b_fox_f, w_gla_branch, w_fox_branch, w_out, g_mix_post, g_xa_pre, g_mem, w_xa_q, w_xa_kv, w_xa_o, g_xa_post, g_ffn_pre, w_ffn_up, w_conv, b_conv, w_ffn_down, g_ffn_post):
    batch, seq, d_model = x.shape
    xf = x.reshape(batch * seq, d_model)
    mf = mem.reshape(-1, d_model)
    per_layer = (g_mix_pre, w_in, b_gate, w_gla_gate_up, b_gla_gate, g_gla_norm, b_fox_f, w_gla_branch,
                 w_fox_branch, w_out, g_mix_post, g_xa_pre, g_mem, w_xa_q, w_xa_kv, w_xa_o, g_xa_post,
                 g_ffn_pre, w_ffn_up, w_conv, b_conv, w_ffn_down, g_ffn_post)
    for l in range(w_in.shape[0]):
        xf = _layer(xf, mf, *(p[l] for p in per_layer), batch=batch, seq=seq)
    return xf.reshape(batch, seq, d_model)
```

```python
import functools
import math

import jax
import jax.numpy as jnp
from jax import lax
from jax.experimental import pallas as pl
from jax.experimental.pallas import tpu as pltpu

F32 = jnp.float32
BF16 = jnp.bfloat16

LANES = 128
SUBLANES = 8
BF16_SUBLANES = 16
V7X_VMEM_BYTES = 64 * 1024 * 1024
VMEM_CAP_BYTES = V7X_VMEM_BYTES - 8 * 1024 * 1024

EPS = 1e-6
GLA_TAU = 16.0
GLA_CHUNK = 64
GLA_SUB = 8
GLA_RANK = 16
GLA_GROUP = 8
GLA_ROWS = 1024
FOX_GROUP = 16
FOX_PREP_GROUP = 8
FOX_EXTRA = 6
XA_HEADS = 4
GELU_C = math.sqrt(2.0 / math.pi)
LOG2E = math.log2(math.e)
NEG_BIG = -1e30


def _params(semantics, block_bytes):
    limit = min(VMEM_CAP_BYTES, 2 * block_bytes + 16 * 1024 * 1024)
    return pltpu.CompilerParams(dimension_semantics=semantics, vmem_limit_bytes=int(limit))


def _nbytes(shape, dtype):
    return math.prod(shape) * jnp.dtype(dtype).itemsize


def _pick_block(target, *dims):
    g = math.gcd(*dims)
    best = None
    for b in range(LANES, min(target, g) + 1, LANES):
        if g % b == 0:
            best = b
    assert best is not None, (target, dims)
    return best


def _aligned_group(target, heads, width, *cols):
    group = target
    while heads % group or any(c % (group * width) for c in cols):
        group //= 2
    return group


def _log_sigmoid(z):
    return jnp.minimum(z, 0.0) - jnp.log(1.0 + jnp.exp(-jnp.abs(z)))


def _sigmoid(z):
    return 1.0 / (1.0 + jnp.exp(-z))


def _split3(x):
    hi = x.astype(BF16)
    r1 = x - hi.astype(F32)
    mid = r1.astype(BF16)
    lo = (r1 - mid.astype(F32)).astype(BF16)
    return hi, mid, lo


def _tril_cumsum(tril, x):
    hi, mid, lo = _split3(x)
    dot = functools.partial(jnp.dot, preferred_element_type=F32)
    return dot(tril, hi) + dot(tril, mid) + dot(tril, lo)


def _dot_nt(a, b):
    return lax.dot_general(a, b, (((1,), (1,)), ((), ())), preferred_element_type=F32)


def _dot_tn(a, b):
    return lax.dot_general(a, b, (((0,), (0,)), ((), ())), preferred_element_type=F32)


def _rms_scale(y):
    return y * lax.rsqrt(jnp.mean(y * y, axis=-1, keepdims=True) + EPS)


def _rmsnorm_side_kernel(x_ref, g_ref, wt_ref, o_ref, side_ref):
    h = (_rms_scale(x_ref[...]) * g_ref[...]).astype(o_ref.dtype)
    o_ref[...] = h
    side_ref[...] = _dot_nt(h, wt_ref[...])


def _rmsnorm_side(x, g, wt, bm=256):
    m, d = x.shape
    n = wt.shape[0]
    blk = _nbytes((bm, d), F32) + _nbytes((bm, d), BF16) + _nbytes((n, d), BF16) + _nbytes((bm, n), F32)
    return pl.pallas_call(
        _rmsnorm_side_kernel,
        grid=(m // bm,),
        in_specs=[pl.BlockSpec((bm, d), lambda i: (i, 0)), pl.BlockSpec((1, d), lambda i: (0, 0)),
                  pl.BlockSpec((n, d), lambda i: (0, 0))],
        out_specs=[pl.BlockSpec((bm, d), lambda i: (i, 0)), pl.BlockSpec((bm, n), lambda i: (i, 0))],
        out_shape=[jax.ShapeDtypeStruct((m, d), BF16), jax.ShapeDtypeStruct((m, n), F32)],
        compiler_params=_params(("parallel",), blk),
        name="rmsnorm",
    )(x, g.reshape(1, d), wt)


def _norm_resid_kernel(y_ref, x_ref, g_ref, o_ref):
    o_ref[...] = x_ref[...] + _rms_scale(y_ref[...]) * g_ref[...]


def _norm_resid(y, x, g, bm=256):
    m, d = x.shape
    blk = 3 * _nbytes((bm, d), F32)
    row = pl.BlockSpec((bm, d), lambda i: (i, 0))
    return pl.pallas_call(
        _norm_resid_kernel,
        grid=(m // bm,),
        in_specs=[row, row, pl.BlockSpec((1, d), lambda i: (0, 0))],
        out_specs=row,
        out_shape=jax.ShapeDtypeStruct((m, d), F32),
        compiler_params=_params(("parallel",), blk),
        name="norm_resid",
    )(y, x, g.reshape(1, d))


def _mm_kernel(a_ref, w_ref, o_ref):
    o_ref[...] = jnp.dot(a_ref[...], w_ref[...], preferred_element_type=F32).astype(o_ref.dtype)


def _matmul(a, w, out_dtype, bm, bn, name):
    m, k = a.shape
    n = w.shape[1]
    bm, bn = _pick_block(bm, m), _pick_block(bn, n)
    blk = _nbytes((bm, k), a.dtype) + _nbytes((k, bn), w.dtype) + _nbytes((bm, bn), out_dtype)
    return pl.pallas_call(
        _mm_kernel,
        grid=(m // bm, n // bn),
        in_specs=[pl.BlockSpec((bm, k), lambda i, j: (i, 0)), pl.BlockSpec((k, bn), lambda i, j: (0, j))],
        out_specs=pl.BlockSpec((bm, bn), lambda i, j: (i, j)),
        out_shape=jax.ShapeDtypeStruct((m, n), out_dtype),
        compiler_params=_params(("parallel", "arbitrary"), blk),
        name=name,
    )(a, w)


def _in_proj_kernel(row_tab, a_ref, w_ref, s_ref, o_ref):
    o_ref[...] = (_dot_nt(a_ref[...], w_ref[...]) * s_ref[...]).astype(o_ref.dtype)


def _in_proj(a, wt, segments, col_scale, bm, bn, name):
    m, k = a.shape
    bm = _pick_block(bm, m)
    bn = _pick_block(bn, *(rows for _, rows in segments))
    n = sum(rows for _, rows in segments)
    firsts = [first_row + r for first_row, rows in segments for r in range(0, rows, bn)]
    assert all(f % BF16_SUBLANES == 0 for f in firsts)
    row_tab = jnp.asarray([f // BF16_SUBLANES for f in firsts], jnp.int32)

    grid_spec = pltpu.PrefetchScalarGridSpec(
        num_scalar_prefetch=1,
        grid=(m // bm, n // bn),
        in_specs=[pl.BlockSpec((bm, k), lambda i, j, tab: (i, 0)),
                  pl.BlockSpec((pl.Element(bn), pl.Element(k)), lambda i, j, tab: (tab[j] * BF16_SUBLANES, 0)),
                  pl.BlockSpec((1, bn), lambda i, j, tab: (0, j))],
        out_specs=pl.BlockSpec((bm, bn), lambda i, j, tab: (i, j)),
    )
    blk = _nbytes((bm, k), a.dtype) + _nbytes((bn, k), wt.dtype) + _nbytes((bm, bn), BF16)
    return pl.pallas_call(
        _in_proj_kernel,
        grid_spec=grid_spec,
        out_shape=jax.ShapeDtypeStruct((m, n), BF16),
        compiler_params=_params(("parallel", "arbitrary"), blk),
        name=name,
    )(row_tab, a, wt, col_scale)


def _norm_mm_kernel(x_ref, g_ref, w_ref, o_ref, h_scr):
    @pl.when(pl.program_id(1) == 0)
    def _():
        h_scr[...] = (_rms_scale(x_ref[...]) * g_ref[...]).astype(h_scr.dtype)

    o_ref[...] = jnp.dot(h_scr[...], w_ref[...], preferred_element_type=F32).astype(o_ref.dtype)


def _norm_matmul(x, g, w, out_dtype, bm, bn, name):
    m, k = x.shape
    n = w.shape[1]
    bm, bn = _pick_block(bm, m), _pick_block(bn, n)
    blk = (_nbytes((bm, k), F32) + _nbytes((k, bn), w.dtype) + _nbytes((bm, bn), out_dtype)
           + _nbytes((bm, k), BF16) // 2)
    return pl.pallas_call(
        _norm_mm_kernel,
        grid=(m // bm, n // bn),
        in_specs=[pl.BlockSpec((bm, k), lambda i, j: (i, 0)), pl.BlockSpec((1, k), lambda i, j: (0, 0)),
                  pl.BlockSpec((k, bn), lambda i, j: (0, j))],
        out_specs=pl.BlockSpec((bm, bn), lambda i, j: (i, j)),
        out_shape=jax.ShapeDtypeStruct((m, n), out_dtype),
        scratch_shapes=[pltpu.VMEM((bm, k), BF16)],
        compiler_params=_params(("parallel", "arbitrary"), blk),
        name=name,
    )(x, g.reshape(1, k), w)


def _mm_norm_resid_kernel(a_ref, w_ref, x_ref, g_ref, *rest):
    o_ref = rest[-2] if len(rest) == 3 else rest[0]
    j = pl.program_id(1)
    bn = w_ref.shape[1]
    col = pl.multiple_of(j * bn, bn)
    o_ref[:, pl.ds(col, bn)] = jnp.dot(a_ref[...], w_ref[...], preferred_element_type=F32)

    @pl.when(j == pl.num_programs(1) - 1)
    def _():
        y = x_ref[...] + _rms_scale(o_ref[...]) * g_ref[...]
        o_ref[...] = y
        if len(rest) == 3:
            next_gain_ref, _, h_ref = rest
            h_ref[...] = (_rms_scale(y) * next_gain_ref[...]).astype(h_ref.dtype)


def _matmul_norm_resid(a, w, x, g, bm, bn, name, next_gain=None):
    m, k = a.shape
    n = w.shape[1]
    bm, bn = _pick_block(bm, m), _pick_block(bn, n)
    row = pl.BlockSpec((bm, n), lambda i, j: (i, 0))
    vec = pl.BlockSpec((1, n), lambda i, j: (0, 0))
    in_specs = [pl.BlockSpec((bm, k), lambda i, j: (i, 0)), pl.BlockSpec((k, bn), lambda i, j: (0, j)), row, vec]
    args = [a, w, x, g.reshape(1, n)]
    out_specs, out_shape = row, jax.ShapeDtypeStruct((m, n), F32)
    blk = _nbytes((bm, k), a.dtype) + _nbytes((k, bn), w.dtype) + 2 * _nbytes((bm, n), F32)
    if next_gain is not None:
        in_specs.append(vec)
        args.append(next_gain.reshape(1, n))
        out_specs, out_shape = [row, row], [out_shape, jax.ShapeDtypeStruct((m, n), BF16)]
        blk += _nbytes((bm, n), BF16)
    return pl.pallas_call(
        _mm_norm_resid_kernel,
        grid=(m // bm, n // bn),
        in_specs=in_specs,
        out_specs=out_specs,
        out_shape=out_shape,
        compiler_params=_params(("parallel", "arbitrary"), blk),
        name=name,
    )(*args)


def _fox_bias_kernel(sm_ref, bias_ref, ct_ref):
    blk = LANES
    n_blk = sm_ref.shape[0] // blk
    r_i = lax.broadcasted_iota(jnp.int32, (blk, blk), 0)
    c_i = lax.broadcasted_iota(jnp.int32, (blk, blk), 1)
    tril = (r_i >= c_i).astype(BF16)

    def body(i, carry):
        start = pl.multiple_of(i * blk, blk)
        z = sm_ref[pl.ds(start, blk), :] + bias_ref[...]
        cs = _tril_cumsum(tril, _log_sigmoid(z)) + carry
        ct_ref[pl.ds(start, blk), :] = cs
        return cs[blk - 1:blk, :]

    lax.fori_loop(0, n_blk, body, jnp.zeros((1, blk), F32))


def _fox_bias(small, bias_row, batch, seq):
    tok = pl.BlockSpec((seq, LANES), lambda b: (b, 0))
    return pl.pallas_call(
        _fox_bias_kernel,
        grid=(batch,),
        in_specs=[tok, pl.BlockSpec((1, LANES), lambda b: (0, 0))],
        out_specs=tok,
        out_shape=jax.ShapeDtypeStruct((batch * seq, LANES), F32),
        compiler_params=_params(("parallel",), 2 * _nbytes((seq, LANES), F32)),
        name="fox_bias",
    )(small, bias_row)


def _fox_prep_kernel(q_ref, k_ref, v_ref, cum_ref, qx_ref, kx_ref, vt_ref, *, group, dh):
    rows = q_ref.shape[0]
    lane_c = lax.broadcasted_iota(jnp.int32, cum_ref.shape, 1)
    lane = lax.broadcasted_iota(jnp.int32, (rows, dh), 1)
    cum = cum_ref[...]
    for g in range(group):
        head = pl.program_id(1) * group + g
        c = jnp.sum(jnp.where(lane_c == head, cum, 0.0), axis=1, keepdims=True) * LOG2E
        hi, mid, lo = (t.astype(F32) for t in _split3(c))
        eq = jnp.where(lane == 0, hi, jnp.where(lane == 1, mid, jnp.where(lane == 2, lo,
                       jnp.where(lane < FOX_EXTRA, 1.0, 0.0))))
        ek = jnp.where(lane < 3, 1.0, jnp.where(lane == 3, -hi, jnp.where(lane == 4, -mid,
                       jnp.where(lane == 5, -lo, 0.0))))
        src = slice(g * dh, (g + 1) * dh)
        qx_ref[:, 2 * g * dh:(2 * g + 1) * dh] = q_ref[:, src]
        qx_ref[:, (2 * g + 1) * dh:(2 * g + 2) * dh] = eq.astype(BF16)
        kx_ref[:, 2 * g * dh:(2 * g + 1) * dh] = k_ref[:, src]
        kx_ref[:, (2 * g + 1) * dh:(2 * g + 2) * dh] = ek.astype(BF16)
        vt_ref[g, :dh, :] = v_ref[:, src].astype(F32).T.astype(BF16)
        vt_ref[g, dh:, :] = jnp.ones((vt_ref.shape[1] - dh, rows), BF16)


def _fox_prep(proj, cum_tok, batch, seq, heads, dh, col_q, col_k, col_v, rows=1024):
    t = proj.shape[0]
    group = _aligned_group(FOX_PREP_GROUP, heads, dh, col_q, col_k, col_v)
    rows = _pick_block(rows, seq)
    per_seq = seq // rows
    gd = group * dh
    cq0, ck0, cv0 = col_q // gd, col_k // gd, col_v // gd
    aug = jax.ShapeDtypeStruct((t, heads * 2 * dh), BF16)
    vrows = dh + BF16_SUBLANES
    blk = (3 * _nbytes((rows, gd), BF16) + _nbytes((rows, LANES), F32) + 2 * _nbytes((rows, 2 * gd), BF16)
           + _nbytes((group, vrows, rows), BF16) + 4 * _nbytes((rows, dh), F32))
    col = lambda c0: pl.BlockSpec((rows, gd), lambda i, h: (i, c0 + h))
    return pl.pallas_call(
        functools.partial(_fox_prep_kernel, group=group, dh=dh),
        grid=(t // rows, heads // group),
        in_specs=[col(cq0), col(ck0), col(cv0), pl.BlockSpec((rows, LANES), lambda i, h: (i, 0))],
        out_specs=[pl.BlockSpec((rows, 2 * gd), lambda i, h: (i, h)),
                   pl.BlockSpec((rows, 2 * gd), lambda i, h: (i, h)),
                   pl.BlockSpec((None, group, vrows, rows), lambda i, h: (i // per_seq, h, 0, i % per_seq))],
        out_shape=[aug, aug, jax.ShapeDtypeStruct((batch, heads, vrows, seq), BF16)],
        compiler_params=_params(("parallel", "arbitrary"), blk),
        name="fox_prep",
    )(proj, proj, proj, cum_tok)


def _fox_kernel(qi_tab, kj_tab, qx_ref, kx_ref, vt_ref, o_ref, m_scr, acc_scr, *, group, dh):
    step = pl.program_id(2)
    qi = qi_tab[step]
    kj = kj_tab[step]
    bq = qx_ref.shape[0]
    bk = kx_ref.shape[0]
    dx = 2 * dh

    @pl.when(kj == 0)
    def _():
        m_scr[...] = jnp.full_like(m_scr, NEG_BIG)
        acc_scr[...] = jnp.zeros_like(acc_scr)

    def update(masked):
        scores = [_dot_nt(kx_ref[:, g * dx:(g + 1) * dx], qx_ref[:, g * dx:(g + 1) * dx])
                  for g in range(group)]
        for g in range(group):
            z = scores[g]
            if masked:
                key = lax.broadcasted_iota(jnp.int32, (bk, bq), 0)
                qry = lax.broadcasted_iota(jnp.int32, (bk, bq), 1)
                z = jnp.where(key <= qry, z, NEG_BIG)
            m_prev = m_scr[g]
            m_new = jnp.maximum(m_prev, jnp.max(z, axis=0, keepdims=True))
            p = jnp.exp2(z - m_new).astype(BF16)
            acc_scr[g] = (jnp.exp2(m_prev - m_new) * acc_scr[g]
                          + jnp.dot(vt_ref[g], p, preferred_element_type=F32))
            m_scr[g] = m_new

    @pl.when(kj != qi)
    def _():
        update(False)

    @pl.when(kj == qi)
    def _():
        update(True)
        for g in range(group):
            acc = acc_scr[g]
            o_ref[:, g * dh:(g + 1) * dh] = (acc[:dh] / acc[dh:dh + 1]).T.astype(o_ref.dtype)


def _fox_attention(qx, kx, vt, batch, seq, heads, dh, bq=512):
    group = _aligned_group(FOX_GROUP, heads, dh)
    nq = seq // bq
    pairs = [(i, j) for i in range(nq) for j in range(i + 1)]
    qi_tab = jnp.asarray([p[0] for p in pairs], jnp.int32)
    kj_tab = jnp.asarray([p[1] for p in pairs], jnp.int32)
    dxg = group * 2 * dh
    vrows = vt.shape[2]

    grid_spec = pltpu.PrefetchScalarGridSpec(
        num_scalar_prefetch=2,
        grid=(batch, heads // group, len(pairs)),
        in_specs=[
            pl.BlockSpec((bq, dxg), lambda b, h, s, qt, kt: (b * nq + qt[s], h)),
            pl.BlockSpec((bq, dxg), lambda b, h, s, qt, kt: (b * nq + kt[s], h)),
            pl.BlockSpec((None, group, vrows, bq), lambda b, h, s, qt, kt: (b, h, 0, kt[s])),
        ],
        out_specs=pl.BlockSpec((bq, group * dh), lambda b, h, s, qt, kt: (b * nq + qt[s], h)),
        scratch_shapes=[pltpu.VMEM((group, 1, bq), F32), pltpu.VMEM((group, vrows, bq), F32)],
    )
    blk = (2 * _nbytes((bq, dxg), BF16) + _nbytes((group, vrows, bq), BF16) + _nbytes((bq, group * dh), BF16)
           + 6 * _nbytes((bq, bq), F32))
    return pl.pallas_call(
        functools.partial(_fox_kernel, group=group, dh=dh),
        grid_spec=grid_spec,
        out_shape=jax.ShapeDtypeStruct((batch * seq, heads * dh), BF16),
        compiler_params=_params(("parallel", "parallel", "arbitrary"), blk),
        name="fox_attention",
    )(qi_tab, kj_tab, qx, kx, vt)


def _gla_kernel(q_ref, k_ref, v_ref, r_ref, low_ref, wup_ref, bup_ref, gn_ref, o_ref, st_ref, kf_ref, b_ref,
                ball_ref, *, group, dk, dv):
    c, sb = GLA_CHUNK, GLA_SUB
    n_sub = c // sb
    rows_per_step = q_ref.shape[0]
    scale = dk ** -0.5

    @pl.when(pl.program_id(2) == 0)
    def _():
        st_ref[...] = jnp.zeros_like(st_ref)

    r_i = lax.broadcasted_iota(jnp.int32, (c, c), 0)
    c_i = lax.broadcasted_iota(jnp.int32, (c, c), 1)
    tril = (r_i >= c_i).astype(BF16)
    sub_shift = sb.bit_length() - 1
    sub_row = jnp.right_shift(r_i, sub_shift)
    sub_col = jnp.right_shift(c_i, sub_shift)
    lane_sb = lax.broadcasted_iota(jnp.int32, (sb, c), 1)
    row_sb = lax.broadcasted_iota(jnp.int32, (sb, c), 0)

    heads = range(group)
    kcols = [slice(g * dk, (g + 1) * dk) for g in heads]
    vcols = [slice(g * dv, (g + 1) * dv) for g in heads]

    z_all = jnp.dot(low_ref[...].astype(BF16), wup_ref[...], preferred_element_type=F32) + bup_ref[...]
    terms = _split3(_log_sigmoid(z_all) * (LOG2E / GLA_TAU))
    for ci in range(rows_per_step // c):
        rows = slice(ci * c, (ci + 1) * c)
        hi, mid, lo = (jnp.dot(tril, t[rows, :], preferred_element_type=F32) for t in terms)
        ball_ref[rows, :] = hi + mid + lo

    def chunk(ci, carry):
        rows = pl.ds(pl.multiple_of(ci * c, c), c)
        b = [ball_ref[rows, kcols[g]] for g in heads]
        qf = [q_ref[rows, kcols[g]].astype(F32) * scale for g in heads]
        kf = [k_ref[rows, kcols[g]].astype(F32) for g in heads]
        v = [v_ref[rows, vcols[g]] for g in heads]
        st = [st_ref[g] for g in heads]

        o = [_dot_nt((qf[g] * jnp.exp2(b[g])).astype(BF16), st[g].astype(BF16)) for g in heads]

        r = []
        for g in heads:
            kf_ref[g] = kf[g]
            b_ref[g] = b[g]
            ends = [jnp.broadcast_to(b_ref[g, (j + 1) * sb - 1:(j + 1) * sb, :], (c, dk)) for j in range(n_sub)]
            end_of_row = jnp.concatenate([e[:sb] for e in ends], axis=0)
            k_hat = (kf[g] * jnp.exp2(end_of_row - b[g])).astype(BF16)
            q_stack = jnp.concatenate(
                [(qf[g] * jnp.exp2(b[g] - ends[j])).astype(BF16) for j in range(n_sub - 1)],
                axis=0)
            r.append(_dot_nt(q_stack, k_hat))

        att = []
        for g in heads:
            a = jnp.zeros((c, c), F32)
            for j in range(n_sub - 1):
                a = jnp.where((sub_col == j) & (sub_row > j), r[g][j * c:(j + 1) * c, :], a)
            diag = []
            for i in range(n_sub):
                q_i = qf[g][i * sb:(i + 1) * sb, :]
                b_i = b[g][i * sb:(i + 1) * sb, :]
                tile = jnp.zeros((sb, c), F32)
                for j in range(i * sb, (i + 1) * sb):
                    dec = jnp.exp2(b_i - b_ref[g, j:j + 1, :])
                    s = jnp.sum(q_i * kf_ref[g, j:j + 1, :] * dec, axis=-1, keepdims=True)
                    tile = jnp.where(lane_sb == j, s, tile)
                diag.append(jnp.where(lane_sb - i * sb <= row_sb, tile, 0.0))
            att.append((a + jnp.concatenate(diag, axis=0)).astype(BF16))

        for g in heads:
            b_last = b[g][c - 1:c, :]
            o[g] = o[g] + jnp.dot(att[g], v[g], preferred_element_type=F32)
            k_dec = (kf[g] * jnp.exp2(b_last - b[g])).astype(BF16)
            st_ref[g] = st[g] * jnp.exp2(b_last) + _dot_tn(v[g], k_dec)

        for g in heads:
            gate = r_ref[rows, vcols[g]].astype(F32)
            o_ref[rows, vcols[g]] = (_rms_scale(o[g]) * gn_ref[...] * (gate * _sigmoid(gate))).astype(o_ref.dtype)
        return carry

    lax.fori_loop(0, rows_per_step // c, chunk, 0)


def _gla(proj, small, wup_pad, b_up, g_norm, batch, seq, heads, dk, dv, col_q, col_k, col_v, col_r):
    group = _aligned_group(GLA_GROUP, heads, dk, col_q, col_k)
    group = _aligned_group(group, heads, dv, col_v, col_r)
    ts = min(GLA_ROWS, seq)
    nt = seq // ts
    gk, gv = group * dk, group * dv
    cq0, ck0, cv0, cr0 = col_q // gk, col_k // gk, col_v // gv, col_r // gv
    blk = 2 * _nbytes((ts, gk), BF16) + 3 * _nbytes((ts, gv), BF16) + _nbytes((ts, LANES), F32)
    return pl.pallas_call(
        functools.partial(_gla_kernel, group=group, dk=dk, dv=dv),
        grid=(batch, heads // group, nt),
        in_specs=[
            pl.BlockSpec((ts, gk), lambda b, h, t: (b * nt + t, cq0 + h)),
            pl.BlockSpec((ts, gk), lambda b, h, t: (b * nt + t, ck0 + h)),
            pl.BlockSpec((ts, gv), lambda b, h, t: (b * nt + t, cv0 + h)),
            pl.BlockSpec((ts, gv), lambda b, h, t: (b * nt + t, cr0 + h)),
            pl.BlockSpec((ts, LANES), lambda b, h, t: (b * nt + t, 0)),
            pl.BlockSpec((LANES, gk), lambda b, h, t: (0, h)),
            pl.BlockSpec((1, gk), lambda b, h, t: (0, h)),
            pl.BlockSpec((1, dv), lambda b, h, t: (0, 0)),
        ],
        out_specs=pl.BlockSpec((ts, gv), lambda b, h, t: (b * nt + t, h)),
        out_shape=jax.ShapeDtypeStruct((batch * seq, heads * dv), BF16),
        scratch_shapes=[pltpu.VMEM((group, dv, dk), F32), pltpu.VMEM((group, GLA_CHUNK, dk), F32),
                        pltpu.VMEM((group, GLA_CHUNK, dk), F32), pltpu.VMEM((ts, gk), F32)],
        compiler_params=_params(("parallel", "parallel", "arbitrary"), blk),
        name="gla",
    )(proj, proj, proj, proj, small, wup_pad, b_up, g_norm)


def _merge_kernel(a_ref, wa_ref, f_ref, wf_ref, g0_ref, g1_ref, b0_ref, b1_ref, o_ref):
    ya = jnp.dot(a_ref[...], wa_ref[...], preferred_element_type=F32)
    yb = jnp.dot(f_ref[...], wf_ref[...], preferred_element_type=F32)
    g0 = _sigmoid(g0_ref[...].astype(F32) + b0_ref[...])
    g1 = _sigmoid(g1_ref[...].astype(F32) + b1_ref[...])
    o_ref[...] = (g0 * ya + g1 * yb).astype(o_ref.dtype)


def _merge(o_gla, w_a, o_fox, w_f, proj, b_gate, col_gates, bm=1024, bn=1024):
    m, ka = o_gla.shape
    kf = o_fox.shape[1]
    d = w_a.shape[1]
    bm, bn = _pick_block(bm, m), _pick_block(bn, d, col_gates)
    g0, nb = col_gates // bn, d // bn
    blk = (_nbytes((bm, ka), BF16) + _nbytes((bm, kf), BF16) + _nbytes((ka, bn), BF16)
           + _nbytes((kf, bn), BF16) + 3 * _nbytes((bm, bn), BF16))
    return pl.pallas_call(
        _merge_kernel,
        grid=(m // bm, nb),
        in_specs=[
            pl.BlockSpec((bm, ka), lambda i, j: (i, 0)),
            pl.BlockSpec((ka, bn), lambda i, j: (0, j)),
            pl.BlockSpec((bm, kf), lambda i, j: (i, 0)),
            pl.BlockSpec((kf, bn), lambda i, j: (0, j)),
            pl.BlockSpec((bm, bn), lambda i, j: (i, g0 + j)),
            pl.BlockSpec((bm, bn), lambda i, j: (i, g0 + nb + j)),
            pl.BlockSpec((1, bn), lambda i, j: (0, j)),
            pl.BlockSpec((1, bn), lambda i, j: (0, nb + j)),
        ],
        out_specs=pl.BlockSpec((bm, bn), lambda i, j: (i, j)),
        out_shape=jax.ShapeDtypeStruct((m, d), BF16),
        compiler_params=_params(("parallel", "arbitrary"), blk),
        name="branch_merge",
    )(o_gla, w_a, o_fox, w_f, proj, proj, b_gate, b_gate)


def _xa_layer_kernel(x_ref, gpre_ref, wq_ref, kv_ref, wo_ref, gpost_ref, gnext_ref, o_ref, h_ref, *, heads, dh):
    x = x_ref[...]
    hq = (_rms_scale(x) * gpre_ref[...]).astype(BF16)
    q = jnp.dot(hq, wq_ref[...], preferred_element_type=F32).astype(BF16)
    outs = []
    for h in range(heads):
        k = kv_ref[:, h * dh:(h + 1) * dh]
        v = kv_ref[:, (heads + h) * dh:(heads + h + 1) * dh]
        s = _dot_nt(q[:, h * dh:(h + 1) * dh], k) * (dh ** -0.5)
        p = jnp.exp(s - jnp.max(s, axis=-1, keepdims=True))
        l = jnp.sum(p, axis=-1, keepdims=True)
        outs.append((jnp.dot(p.astype(BF16), v, preferred_element_type=F32) / l).astype(BF16))
    y = jnp.dot(jnp.concatenate(outs, axis=1), wo_ref[...], preferred_element_type=F32)
    z = x + _rms_scale(y) * gpost_ref[...]
    o_ref[...] = z
    h_ref[...] = (_rms_scale(z) * gnext_ref[...]).astype(h_ref.dtype)


def _xa_layer(x, g_pre, w_q, kv, w_o, g_post, g_next, batch, seq, mem_len, heads, dh, bm=256):
    m, d = x.shape
    w = heads * dh
    per_seq = seq // bm
    row = pl.BlockSpec((bm, d), lambda b, i: (b * per_seq + i, 0))
    vec = pl.BlockSpec((1, d), lambda b, i: (0, 0))
    once = lambda shape: pl.BlockSpec(shape, lambda b, i: (0, 0), pipeline_mode=pl.Buffered(1))
    blk = (2 * _nbytes((bm, d), F32) + _nbytes((bm, d), BF16) + _nbytes((mem_len, 2 * w), BF16)
           + (_nbytes((d, w), BF16) + _nbytes((w, d), BF16)) // 2 + 2 * _nbytes((bm, d), F32))
    return pl.pallas_call(
        functools.partial(_xa_layer_kernel, heads=heads, dh=dh),
        grid=(batch, per_seq),
        in_specs=[row, vec, once((d, w)), pl.BlockSpec((mem_len, 2 * w), lambda b, i: (b, 0)), once((w, d)),
                  vec, vec],
        out_specs=[row, row],
        out_shape=[jax.ShapeDtypeStruct((m, d), F32), jax.ShapeDtypeStruct((m, d), BF16)],
        compiler_params=_params(("parallel", "arbitrary"), blk),
        name="xa_layer",
    )(x, g_pre.reshape(1, d), w_q, kv, w_o, g_post.reshape(1, d), g_next.reshape(1, d))


def _ffn_up_kernel(a_ref, wg_ref, wu_ref, cg_ref, cu_ref, bg_ref, bu_ref, o_ref, w_scr, raw_ref,
                   *, blocks_per_seq):
    i = pl.program_id(1)
    bm = a_ref.shape[0]

    @pl.when(i == 0)
    def _():
        w_scr[0] = wg_ref[...].astype(BF16)
        w_scr[1] = wu_ref[...].astype(BF16)

    @pl.when(lax.rem(i, blocks_per_seq) == 0)
    def _():
        raw_ref[:, :SUBLANES, :] = jnp.zeros((2, SUBLANES, raw_ref.shape[2]), F32)

    a = a_ref[...]

    def conv(slot, c_ref, b_ref):
        raw_ref[slot, SUBLANES:, :] = jnp.dot(a, w_scr[slot], preferred_element_type=F32)
        x0 = raw_ref[slot, SUBLANES:, :]
        x1 = raw_ref[slot, SUBLANES - 1:SUBLANES - 1 + bm, :]
        x2 = raw_ref[slot, SUBLANES - 2:SUBLANES - 2 + bm, :]
        raw_ref[slot, :SUBLANES, :] = raw_ref[slot, bm:, :]
        return c_ref[0:1, :] * x2 + c_ref[1:2, :] * x1 + c_ref[2:3, :] * x0 + b_ref[...]

    g = conv(0, cg_ref, bg_ref)
    u = conv(1, cu_ref, bu_ref)
    gelu = (0.5 * g) * (1.0 + jnp.tanh(g * (GELU_C + (GELU_C * 0.044715) * (g * g))))
    o_ref[...] = (gelu * u).astype(o_ref.dtype)


def _ffn_up(h, w_up, w_conv, b_conv, seq, d_ff, bm=1024, bn=256):
    m, k = h.shape
    bm, bn = _pick_block(bm, seq), _pick_block(bn, d_ff)
    nb = d_ff // bn
    taps = w_conv.shape[0]
    wsp = lambda off: pl.BlockSpec((k, bn), lambda j, i: (0, off + j))
    csp = lambda off: pl.BlockSpec((taps, bn), lambda j, i: (0, off + j))
    bsp = lambda off: pl.BlockSpec((1, bn), lambda j, i: (0, off + j))
    scratch = 2 * _nbytes((k, bn), BF16) + 8 * _nbytes((bm, bn), F32)
    blk = _nbytes((bm, k), BF16) + 2 * _nbytes((k, bn), F32) + _nbytes((bm, bn), BF16) + scratch // 2
    return pl.pallas_call(
        functools.partial(_ffn_up_kernel, blocks_per_seq=seq // bm),
        grid=(nb, m // bm),
        in_specs=[pl.BlockSpec((bm, k), lambda j, i: (i, 0)), wsp(0), wsp(nb), csp(0), csp(nb), bsp(0), bsp(nb)],
        out_specs=pl.BlockSpec((bm, bn), lambda j, i: (i, j)),
        out_shape=jax.ShapeDtypeStruct((m, d_ff), BF16),
        scratch_shapes=[pltpu.VMEM((2, k, bn), BF16), pltpu.VMEM((2, SUBLANES + bm, bn), F32)],
        compiler_params=_params(("parallel", "arbitrary"), blk),
        name="ffn_up",
    )(h, w_up, w_up, w_conv, w_conv, b_conv, b_conv)


def _layer(x, mem, g_mix_pre, w_in, b_gate, w_gla_gate_up, b_gla_gate, g_gla_norm, b_fox_f,
           w_gla_branch, w_fox_branch, w_out, g_mix_post, g_xa_pre, g_mem, w_xa_q, w_xa_kv,
           w_xa_o, g_xa_post, g_ffn_pre, w_ffn_up, w_conv, b_conv, w_ffn_down, g_ffn_post,
           batch, seq):
    d_model = x.shape[1]
    gla_qk = w_gla_gate_up.shape[1]
    gla_dv = g_gla_norm.shape[0]
    gla_vw = w_gla_branch.shape[0]
    gla_heads = gla_vw // gla_dv
    gla_dk = gla_qk // gla_heads
    fox_w = w_fox_branch.shape[0]
    fox_heads = b_fox_f.shape[0]
    fox_dh = fox_w // fox_heads
    xa_w = w_xa_q.shape[1]
    mem_len = mem.shape[0] // batch
    xa_heads = XA_HEADS
    xa_dh = xa_w // xa_heads
    d_ff = w_ffn_down.shape[0]
    assert fox_heads + GLA_RANK <= LANES and gla_dk == LANES and fox_dh == LANES

    sizes = (gla_qk, gla_qk, gla_vw, gla_vw, GLA_RANK, fox_w, fox_w, fox_w, fox_heads, 2 * d_model)
    offs = [0]
    for s in sizes:
        offs.append(offs[-1] + s)
    o_gq, o_gk, o_gv, o_gr, o_low, o_fq, o_fk, o_fv, o_ff, o_gates = offs[:-1]
    wt = w_in.T.astype(BF16)
    wide_segments = [(o_gq, o_low - o_gq), (o_fq, o_ff - o_fq), (o_gates, 2 * d_model)]
    w_small = jnp.concatenate([wt[o_ff:o_ff + fox_heads], wt[o_low:o_low + GLA_RANK],
                               jnp.zeros((LANES - fox_heads - GLA_RANK, d_model), BF16)], axis=0)
    c_gq, c_gk, c_gv, c_gr = o_gq, o_gk, o_gv, o_gr
    c_fq = o_low
    c_fk, c_fv = c_fq + fox_w, c_fq + 2 * fox_w
    c_gates = c_fq + 3 * fox_w
    col_scale = jnp.ones((1, c_gates + 2 * d_model), F32).at[:, c_fq:c_fk].set(LOG2E * fox_dh ** -0.5)
    wup_pad = jnp.zeros((LANES, gla_qk), F32).at[fox_heads:fox_heads + GLA_RANK].set(w_gla_gate_up).astype(BF16)
    fox_bias_row = jnp.zeros((1, LANES), F32).at[0, :fox_heads].set(b_fox_f)

    h, small = _rmsnorm_side(x, g_mix_pre, w_small)
    proj = _in_proj(h, wt, wide_segments, col_scale, 1024, 1024, "in_proj")
    o_gla = _gla(proj, small, wup_pad, b_gla_gate.reshape(1, -1), g_gla_norm.reshape(1, -1),
                 batch, seq, gla_heads, gla_dk, gla_dv, c_gq, c_gk, c_gv, c_gr)
    cum_tok = _fox_bias(small, fox_bias_row, batch, seq)
    qx, kx, vt = _fox_prep(proj, cum_tok, batch, seq, fox_heads, fox_dh, c_fq, c_fk, c_fv)
    o_fox = _fox_attention(qx, kx, vt, batch, seq, fox_heads, fox_dh)
    merged = _merge(o_gla, w_gla_branch.astype(BF16), o_fox, w_fox_branch.astype(BF16), proj,
                    b_gate.reshape(1, -1), c_gates)
    x = _matmul_norm_resid(merged, w_out.astype(BF16), x, g_mix_post, 512, 512, "mix_out")

    kv = _norm_matmul(mem, g_mem, w_xa_kv.astype(BF16), BF16, 512, 1024, "xa_kv")
    x, h = _xa_layer(x, g_xa_pre, w_xa_q.astype(BF16), kv, w_xa_o.astype(BF16), g_xa_post, g_ffn_pre,
                     batch, seq, mem_len, xa_heads, xa_dh)

    act = _ffn_up(h, w_ffn_up, w_conv, b_conv.reshape(1, -1), seq, d_ff)
    y = _matmul(act, w_ffn_down.astype(BF16), F32, 512, 512, "ffn_down")
    return _norm_resid(y, x, g_ffn_post)


def kernel(x, mem, g_mix_pre, w_in, b_gate, w_gla_gate_up, b_gla_gate, g_gla_norm, b_fox_f, w_gla_branch, w_fox_branch, w_out, g_mix_post, g_xa_pre, g_mem, w_xa_q, w_xa_kv, w_xa_o, g_xa_post, g_ffn_pre, w_ffn_up, w_conv, b_conv, w_ffn_down, g_ffn_post):
    batch, seq, d_model = x.shape
    xf = x.reshape(batch * seq, d_model)
    mf = mem.reshape(-1, d_model)
    per_layer = (g_mix_pre, w_in, b_gate, w_gla_gate_up, b_gla_gate, g_gla_norm, b_fox_f, w_gla_branch,
                 w_fox_branch, w_out, g_mix_post, g_xa_pre, g_mem, w_xa_q, w_xa_kv, w_xa_o, g_xa_post,
                 g_ffn_pre, w_ffn_up, w_conv, b_conv, w_ffn_down, g_ffn_post)
    for l in range(w_in.shape[0]):
        xf = _layer(xf, mf, *(p[l] for p in per_layer), batch=batch, seq=seq)
    return xf.reshape(batch, seq, d_model)
```
